```python
import jax, jax.numpy as jnp
from jax import lax
import numpy as np

D_MODEL = 2048
BATCH = 4
SEQ = 2048
DEPTH = 4

PLE_DIM = 256
ROPE_THETA = 10000.0
NORM_EPS = 1e-6
Q_BLOCK = 128
NEG = -1e30
FORCE = 1e6

GLA_HEADS = 8
GLA_DK = 64
GLA_DV = 128
GLA_GATE_RANK = 16
GLA_TAU = 16.0
GLA_CHUNK = 64

MLA_HEADS = 8
MLA_Q_RANK = 512
MLA_KV_RANK = 512
MLA_NOPE = 128
MLA_ROPE = 64
MLA_V = 128

NSA_HEADS = 16
NSA_KV_GROUPS = 4
NSA_HPG = NSA_HEADS // NSA_KV_GROUPS
NSA_HEAD_DIM = 128
NSA_CMP_LEN = 32
NSA_CMP_STRIDE = 16
NSA_SEL_LEN = 64
NSA_SEL_TOPK = 16
NSA_WINDOW = 512
NSA_SEL_QCHUNK = 32

FFN_HIDDEN = -(-8 * D_MODEL // (3 * 256)) * 256

AB_SPLITS = (GLA_HEADS * GLA_DK, GLA_HEADS * GLA_DK, GLA_HEADS * GLA_DV, GLA_HEADS * GLA_DV,
             GLA_GATE_RANK, MLA_Q_RANK, MLA_KV_RANK, MLA_ROPE)
AB_IN = sum(AB_SPLITS)
AB_MIX = GLA_HEADS * GLA_DV + MLA_HEADS * MLA_V
NSA_KV_W = NSA_KV_GROUPS * NSA_HEAD_DIM
NSA_SPLITS = (NSA_HEADS * NSA_HEAD_DIM,) + (NSA_KV_W,) * 6 + (3 * NSA_HEADS,)
NSA_IN = sum(NSA_SPLITS)
NSA_MIX = NSA_HEADS * NSA_HEAD_DIM

kernel_name = "hybrid_gla_mla_nsa_sandwich_ple"


def split_cols(z, widths):
    out, off = [], 0
    for w in widths:
        out.append(z[..., off:off + w])
        off += w
    return out


def rms_norm(x, w):
    xf = x.astype(jnp.float32)
    y = xf * lax.rsqrt(jnp.mean(xf * xf, axis=-1, keepdims=True) + NORM_EPS)
    return (y * w.astype(jnp.float32)).astype(x.dtype)


def rope_tables(positions, dim):
    inv = jnp.power(ROPE_THETA, -jnp.arange(0, dim, 2, dtype=jnp.float32) / dim)
    ang = positions.astype(jnp.float32)[..., None] * inv
    return jnp.cos(ang), jnp.sin(ang)


def apply_rope(x, cos, sin):
    half = x.shape[-1] // 2
    xf = x.astype(jnp.float32)
    x1, x2 = xf[..., :half], xf[..., half:]
    return jnp.concatenate([x1 * cos - x2 * sin, x2 * cos + x1 * sin], axis=-1).astype(x.dtype)


def gla_chunked(q, k, v, log_a):
    B, H, S, dk = q.shape
    dv = v.shape[-1]
    L = GLA_CHUNK
    N = S // L
    f32 = jnp.float32
    qf = q.astype(f32).reshape(B, H, N, L, dk) * (dk ** -0.5)
    kf = k.astype(f32).reshape(B, H, N, L, dk)
    vf = v.astype(f32).reshape(B, H, N, L, dv)
    b = jnp.cumsum(log_a.astype(f32).reshape(B, H, N, L, dk), axis=3)
    b_end = b[:, :, :, -1:, :]
    q_dec = qf * jnp.exp(b)
    k_inv = kf * jnp.exp(-b)
    k_end = kf * jnp.exp(b_end - b)
    causal = jnp.tril(jnp.ones((L, L), dtype=bool))
    attn = jnp.where(causal, jnp.einsum('bhnid,bhnjd->bhnij', q_dec, k_inv), 0.0)
    o_intra = jnp.einsum('bhnij,bhnjv->bhniv', attn, vf)
    upd = jnp.einsum('bhnjd,bhnjv->bhndv', k_end, vf)
    decay = jnp.exp(b_end[:, :, :, 0, :])

    def step(state, inp):
        dec, u = inp
        return dec[..., None] * state + u, state

    s0 = jnp.zeros((B, H, dk, dv), f32)
    _, s_prev = lax.scan(step, s0, (jnp.moveaxis(decay, 2, 0), jnp.moveaxis(upd, 2, 0)))
    s_prev = jnp.moveaxis(s_prev, 0, 2)
    o_inter = jnp.einsum('bhnid,bhndv->bhniv', q_dec, s_prev)
    return (o_intra + o_inter).reshape(B, H, S, dv).astype(v.dtype)


def causal_attention_blocks(q, k, v, scale):
    B, H, S, dq = q.shape
    nb = S // Q_BLOCK
    q_b = jnp.moveaxis(q.reshape(B, H, nb, Q_BLOCK, dq), 2, 0)
    kpos = jnp.arange(S)

    def one(args):
        qb, i = args
        s = jnp.einsum('bhqd,bhkd->bhqk', qb, k).astype(jnp.float32) * scale
        qpos = i * Q_BLOCK + jnp.arange(Q_BLOCK)
        s = jnp.where(kpos[None, :] <= qpos[:, None], s, NEG)
        p = jax.nn.softmax(s, axis=-1)
        return jnp.einsum('bhqk,bhkd->bhqd', p.astype(v.dtype), v)

    o = lax.map(one, (q_b, jnp.arange(nb)))
    return jnp.moveaxis(o, 0, 2).reshape(B, H, S, v.shape[-1])


def gla_mla_mixer(h, cos64, sin64, w_in, w_alpha_up, b_alpha, gla_norm_w,
                  q_norm_w, w_uq, kv_norm_w, w_ukv, w_out):
    B, S, _ = h.shape
    z = h @ w_in
    q_g, k_g, v_g, g_g, a_lr, c_q, c_kv, k_r = split_cols(z, AB_SPLITS)

    def heads(t, n):
        return t.reshape(B, S, n, -1).transpose(0, 2, 1, 3)

    log_a = jax.nn.log_sigmoid((a_lr @ w_alpha_up + b_alpha).astype(jnp.float32)) / GLA_TAU
    o_gla = gla_chunked(heads(q_g, GLA_HEADS), heads(k_g, GLA_HEADS), heads(v_g, GLA_HEADS),
                        heads(log_a, GLA_HEADS))
    o_gla = rms_norm(o_gla.transpose(0, 2, 1, 3), gla_norm_w).reshape(B, S, -1) * jax.nn.silu(g_g)

    q = (rms_norm(c_q, q_norm_w) @ w_uq).reshape(B, S, MLA_HEADS, MLA_NOPE + MLA_ROPE).transpose(0, 2, 1, 3)
    kv = (rms_norm(c_kv, kv_norm_w) @ w_ukv).reshape(B, S, MLA_HEADS, MLA_NOPE + MLA_V).transpose(0, 2, 1, 3)
    q_rope = apply_rope(q[..., MLA_NOPE:], cos64[:, None], sin64[:, None])
    k_rope = apply_rope(k_r, cos64, sin64)[:, None]
    qm = jnp.concatenate([q[..., :MLA_NOPE], q_rope], axis=-1)
    km = jnp.concatenate([kv[..., :MLA_NOPE],
                          jnp.broadcast_to(k_rope, (B, MLA_HEADS, S, MLA_ROPE))], axis=-1)
    o_mla = causal_attention_blocks(qm, km, kv[..., MLA_NOPE:], (MLA_NOPE + MLA_ROPE) ** -0.5)
    o_mla = o_mla.transpose(0, 2, 1, 3).reshape(B, S, -1)
    return jnp.concatenate([o_gla, o_mla.astype(o_gla.dtype)], axis=-1) @ w_out


def nsa_compress(t, pos, w1, w2):
    S = t.shape[2]
    nc = (S - NSA_CMP_LEN) // NSA_CMP_STRIDE + 1
    idx = jnp.arange(nc)[:, None] * NSA_CMP_STRIDE + jnp.arange(NSA_CMP_LEN)[None, :]
    blocks = t[:, :, idx, :] + pos
    flat = blocks.reshape(blocks.shape[0], blocks.shape[1], nc, NSA_CMP_LEN * t.shape[-1])
    return jax.nn.gelu(flat @ w1) @ w2


def nsa_compressed_branch(q, kc, vc, cmp_pos, cmp_w1, cmp_w2, scale):
    S = q.shape[3]
    k_cmp = nsa_compress(kc, cmp_pos[0], cmp_w1[0], cmp_w2[0])
    v_cmp = nsa_compress(vc, cmp_pos[1], cmp_w1[1], cmp_w2[1])
    nc = k_cmp.shape[2]
    s = jnp.einsum('bgjsd,bgnd->bgjsn', q, k_cmp).astype(jnp.float32) * scale
    blk_end = jnp.arange(nc) * NSA_CMP_STRIDE + NSA_CMP_LEN - 1
    ok = blk_end[None, :] <= jnp.arange(S)[:, None]
    p = jnp.where(ok, jax.nn.softmax(jnp.where(ok, s, NEG), axis=-1), 0.0)
    o = jnp.einsum('bgjsn,bgnd->bgjsd', p.astype(v_cmp.dtype), v_cmp)
    return o, p


def nsa_select_blocks(p_cmp, S):
    nc = p_cmp.shape[-1]
    ns = S // NSA_SEL_LEN
    c_start = np.arange(nc) * NSA_CMP_STRIDE
    c_end = c_start + NSA_CMP_LEN
    s_start = np.arange(ns) * NSA_SEL_LEN
    s_end = s_start + NSA_SEL_LEN
    overlap = jnp.asarray(((c_start[:, None] < s_end[None, :]) &
                           (c_end[:, None] > s_start[None, :])).astype(np.float32))
    imp = jnp.einsum('bgjsn,nm->bgsm', p_cmp, overlap)
    t = jnp.arange(S)[:, None]
    m = jnp.arange(ns)[None, :]
    causal = m * NSA_SEL_LEN <= t
    cur = t // NSA_SEL_LEN
    forced = (m == 0) | (m == cur) | (m == cur - 1)
    score = jnp.where(causal, jnp.where(forced, FORCE, imp), -FORCE)
    val, idx = lax.top_k(score, min(NSA_SEL_TOPK, ns))
    return idx, val > -0.5 * FORCE


def nsa_selected_branch(q, ks, vs, sel_idx, sel_ok, scale):
    B, G, HG, S, dh = q.shape
    ns = S // NSA_SEL_LEN
    kk = sel_idx.shape[-1]
    Qc = NSA_SEL_QCHUNK
    nq = S // Qc
    k_blocks = ks.reshape(B, G, ns, NSA_SEL_LEN, dh)
    v_blocks = vs.reshape(B, G, ns, NSA_SEL_LEN, dh)
    q_ch = jnp.moveaxis(q.reshape(B, G, HG, nq, Qc, dh), 3, 0)
    i_ch = jnp.moveaxis(sel_idx.reshape(B, G, nq, Qc, kk), 2, 0)
    ok_ch = jnp.moveaxis(sel_ok.reshape(B, G, nq, Qc, kk), 2, 0)
    b_ix = jnp.arange(B)[:, None, None, None]
    g_ix = jnp.arange(G)[None, :, None, None]
    n_keys = kk * NSA_SEL_LEN

    def one(args):
        qc, ic, okc, c = args
        kg = k_blocks[b_ix, g_ix, ic].reshape(B, G, Qc, n_keys, dh)
        vg = v_blocks[b_ix, g_ix, ic].reshape(B, G, Qc, n_keys, dh)
        kpos = ic[..., None] * NSA_SEL_LEN + jnp.arange(NSA_SEL_LEN)
        qpos = c * Qc + jnp.arange(Qc)
        mask = (okc[..., None] & (kpos <= qpos[None, None, :, None, None])).reshape(B, G, Qc, n_keys)
        s = jnp.einsum('bgjqd,bgqkd->bgjqk', qc, kg).astype(jnp.float32) * scale
        p = jax.nn.softmax(jnp.where(mask[:, :, None], s, NEG), axis=-1)
        return jnp.einsum('bgjqk,bgqkd->bgjqd', p.astype(vg.dtype), vg)

    o = lax.map(one, (q_ch, i_ch, ok_ch, jnp.arange(nq)))
    return jnp.moveaxis(o, 0, 3).reshape(B, G, HG, S, dh)


def nsa_window_branch(q, kw, vw, scale):
    B, G, HG, S, dh = q.shape
    nb = S // Q_BLOCK
    span = NSA_WINDOW + Q_BLOCK
    pad = ((0, 0), (0, 0), (NSA_WINDOW, 0), (0, 0))
    k_pad = jnp.pad(kw, pad)
    v_pad = jnp.pad(vw, pad)
    q_b = jnp.moveaxis(q.reshape(B, G, HG, nb, Q_BLOCK, dh), 3, 0)

    def one(args):
        qb, i = args
        start = i * Q_BLOCK
        kb = lax.dynamic_slice_in_dim(k_pad, start, span, axis=2)
        vb = lax.dynamic_slice_in_dim(v_pad, start, span, axis=2)
        kpos = start - NSA_WINDOW + jnp.arange(span)
        qpos = start + jnp.arange(Q_BLOCK)
        mask = ((kpos[None, :] <= qpos[:, None]) & (kpos[None, :] > qpos[:, None] - NSA_WINDOW)
                & (kpos[None, :] >= 0))
        s = jnp.einsum('bgjqd,bgkd->bgjqk', qb, kb).astype(jnp.float32) * scale
        p = jax.nn.softmax(jnp.where(mask, s, NEG), axis=-1)
        return jnp.einsum('bgjqk,bgkd->bgjqd', p.astype(vb.dtype), vb)

    o = lax.map(one, (q_b, jnp.arange(nb)))
    return jnp.moveaxis(o, 0, 3).reshape(B, G, HG, S, dh)


def nsa_mixer(h, cos, sin, w_in, b_gate, cmp_pos, cmp_w1, cmp_w2, w_out):
    B, S, _ = h.shape
    G, HG, dh = NSA_KV_GROUPS, NSA_HPG, NSA_HEAD_DIM
    q, kc, vc, ks, vs, kw, vw, gl = split_cols(h @ w_in, NSA_SPLITS)
    q = apply_rope(q.reshape(B, S, G, HG, dh).transpose(0, 2, 3, 1, 4), cos[:, None, None], sin[:, None, None])

    def kvh(t):
        return t.reshape(B, S, G, dh).transpose(0, 2, 1, 3)

    kc = apply_rope(kvh(kc), cos[:, None], sin[:, None])
    ks = apply_rope(kvh(ks), cos[:, None], sin[:, None])
    kw = apply_rope(kvh(kw), cos[:, None], sin[:, None])
    vc, vs, vw = kvh(vc), kvh(vs), kvh(vw)
    scale = dh ** -0.5
    o_cmp, p_cmp = nsa_compressed_branch(q, kc, vc, cmp_pos, cmp_w1, cmp_w2, scale)
    sel_idx, sel_ok = nsa_select_blocks(p_cmp, S)
    o_sel = nsa_selected_branch(q, ks, vs, sel_idx, sel_ok, scale)
    o_win = nsa_window_branch(q, kw, vw, scale)
    gates = jax.nn.sigmoid((gl + b_gate).astype(jnp.float32)).reshape(B, S, G, HG, 3).transpose(0, 2, 3, 1, 4)
    o = gates[..., 0:1] * o_cmp + gates[..., 1:2] * o_sel + gates[..., 2:3] * o_win
    o = o.transpose(0, 3, 1, 2, 4).reshape(B, S, NSA_MIX).astype(h.dtype)
    return o @ w_out


def swiglu(h, w_gate, w_up, w_down):
    return (jax.nn.silu(h @ w_gate) * (h @ w_up)) @ w_down


def setup_inputs(seed: int = 0) -> dict:
    key = jax.random.key(seed)
    ks = jax.random.split(key, 32)
    f32 = jnp.float32
    NE = (DEPTH + 1) // 2
    NO = DEPTH // 2

    def nrm(k, shape, s):
        return jax.random.normal(k, shape, f32) * s

    def gain(k, shape):
        return 1.0 + 0.05 * jax.random.normal(k, shape, f32)

    return {
        "x": nrm(ks[0], (BATCH, SEQ, D_MODEL), 1.0),
        "p": nrm(ks[1], (DEPTH, BATCH, SEQ, PLE_DIM), 1.0),
        "positions": (jnp.arange(SEQ, dtype=jnp.int32)[None, :]
                      + jax.random.randint(ks[2], (BATCH, 1), 0, 1024, dtype=jnp.int32)),
        "ln_mix_pre": gain(ks[3], (DEPTH, D_MODEL)),
        "ln_mix_post": gain(ks[4], (DEPTH, D_MODEL)),
        "ln_ffn_pre": gain(ks[5], (DEPTH, D_MODEL)),
        "ln_ffn_post": gain(ks[6], (DEPTH, D_MODEL)),
        "ab_w_in": nrm(ks[7], (NE, D_MODEL, AB_IN), D_MODEL ** -0.5),
        "gla_w_alpha_up": nrm(ks[8], (NE, GLA_GATE_RANK, GLA_HEADS * GLA_DK), GLA_GATE_RANK ** -0.5),
        "gla_b_alpha": nrm(ks[9], (NE, GLA_HEADS * GLA_DK), 0.1),
        "gla_norm_w": gain(ks[10], (NE, GLA_DV)),
        "mla_q_norm_w": gain(ks[11], (NE, MLA_Q_RANK)),
        "mla_w_uq": nrm(ks[12], (NE, MLA_Q_RANK, MLA_HEADS * (MLA_NOPE + MLA_ROPE)), MLA_Q_RANK ** -0.5),
        "mla_kv_norm_w": gain(ks[13], (NE, MLA_KV_RANK)),
        "mla_w_ukv": nrm(ks[14], (NE, MLA_KV_RANK, MLA_HEADS * (MLA_NOPE + MLA_V)), MLA_KV_RANK ** -0.5),
        "ab_w_out": nrm(ks[15], (NE, AB_MIX, D_MODEL), AB_MIX ** -0.5),
        "nsa_w_in": nrm(ks[16], (NO, D_MODEL, NSA_IN), D_MODEL ** -0.5),
        "nsa_b_gate": nrm(ks[17], (NO, 3 * NSA_HEADS), 0.1),
        "nsa_cmp_pos": nrm(ks[18], (NO, 2, NSA_CMP_LEN, NSA_HEAD_DIM), 0.1),
        "nsa_cmp_w1": nrm(ks[19], (NO, 2, NSA_CMP_LEN * NSA_HEAD_DIM, NSA_HEAD_DIM),
                          (NSA_CMP_LEN * NSA_HEAD_DIM) ** -0.5),
        "nsa_cmp_w2": nrm(ks[20], (NO, 2, NSA_HEAD_DIM, NSA_HEAD_DIM), NSA_HEAD_DIM ** -0.5),
        "nsa_w_out": nrm(ks[21], (NO, NSA_MIX, D_MODEL), NSA_MIX ** -0.5),
        "ffn_w_gate": nrm(ks[22], (DEPTH, D_MODEL, FFN_HIDDEN), D_MODEL ** -0.5),
        "ffn_w_up": nrm(ks[23], (DEPTH, D_MODEL, FFN_HIDDEN), D_MODEL ** -0.5),
        "ffn_w_down": nrm(ks[24], (DEPTH, FFN_HIDDEN, D_MODEL), FFN_HIDDEN ** -0.5),
        "ple_w_gate": nrm(ks[25], (DEPTH, D_MODEL, D_MODEL), D_MODEL ** -0.5),
        "ple_b_gate": nrm(ks[26], (DEPTH, D_MODEL), 0.01),
        "ple_w_proj": nrm(ks[27], (DEPTH, PLE_DIM, D_MODEL), PLE_DIM ** -0.5),
    }


def reference(x, p, positions, ln_mix_pre, ln_mix_post, ln_ffn_pre, ln_ffn_post,
              ab_w_in, gla_w_alpha_up, gla_b_alpha, gla_norm_w, mla_q_norm_w, mla_w_uq,
              mla_kv_norm_w, mla_w_ukv, ab_w_out,
              nsa_w_in, nsa_b_gate, nsa_cmp_pos, nsa_cmp_w1, nsa_cmp_w2, nsa_w_out,
              ffn_w_gate, ffn_w_up, ffn_w_down, ple_w_gate, ple_b_gate, ple_w_proj):
    cos64, sin64 = rope_tables(positions, MLA_ROPE)
    cos128, sin128 = rope_tables(positions, NSA_HEAD_DIM)
    h = x
    for i in range(DEPTH):
        hn = rms_norm(h, ln_mix_pre[i])
        if i % 2 == 0:
            j = i // 2
            m = gla_mla_mixer(hn, cos64, sin64, ab_w_in[j], gla_w_alpha_up[j], gla_b_alpha[j],
                              gla_norm_w[j], mla_q_norm_w[j], mla_w_uq[j], mla_kv_norm_w[j],
                              mla_w_ukv[j], ab_w_out[j])
        else:
            j = i // 2
            m = nsa_mixer(hn, cos128, sin128, nsa_w_in[j], nsa_b_gate[j], nsa_cmp_pos[j],
                          nsa_cmp_w1[j], nsa_cmp_w2[j], nsa_w_out[j])
        h = h + rms_norm(m, ln_mix_post[i])
        f = swiglu(rms_norm(h, ln_ffn_pre[i]), ffn_w_gate[i], ffn_w_up[i], ffn_w_down[i])
        h = h + rms_norm(f, ln_ffn_post[i])
        gate = jax.nn.sigmoid(h @ ple_w_gate[i] + ple_b_gate[i])
        h = h + gate * (p[i] @ ple_w_proj[i])
    return h
```

```python
import functools

import numpy as np
import jax
import jax.numpy as jnp
from jax import lax
from jax.experimental import pallas as pl
from jax.experimental.pallas import tpu as pltpu

F32 = jnp.float32
BF16 = jnp.bfloat16

D_MODEL = 2048
PLE_DIM = 256
ROPE_THETA = 10000.0
NORM_EPS = 1e-6
NEG = -1e30
FORCE = 1e6

GLA_HEADS = 8
GLA_DK = 64
GLA_DV = 128
GLA_GATE_RANK = 16
GLA_TAU = 16.0
GLA_CHUNK = 64

MLA_HEADS = 8
MLA_Q_RANK = 512
MLA_KV_RANK = 512
MLA_NOPE = 128
MLA_ROPE = 64
MLA_V = 128
MLA_QK = MLA_NOPE + MLA_ROPE

NSA_HEADS = 16
NSA_KV_GROUPS = 4
NSA_HPG = NSA_HEADS // NSA_KV_GROUPS
NSA_HEAD_DIM = 128
NSA_CMP_LEN = 32
NSA_CMP_STRIDE = 16
NSA_SEL_LEN = 64
NSA_SEL_TOPK = 16
NSA_WINDOW = 512

FFN_HIDDEN = 5632

AB_SPLITS = (GLA_HEADS * GLA_DK, GLA_HEADS * GLA_DK, GLA_HEADS * GLA_DV, GLA_HEADS * GLA_DV,
             GLA_GATE_RANK, MLA_Q_RANK, MLA_KV_RANK, MLA_ROPE)
NSA_KV_W = NSA_KV_GROUPS * NSA_HEAD_DIM
NSA_SPLITS = (NSA_HEADS * NSA_HEAD_DIM,) + (NSA_KV_W,) * 6 + (3 * NSA_HEADS,)

LANE = 128
VMEM_LIMIT = 56 * 1024 * 1024

AB_Z = 4608
AB_BLK_QK = 0
AB_BLK_V = 8
AB_BLK_G = 16
AB_BLK_CQ = 24
AB_BLK_CKV = 28
AB_BLK_KR = 32
AB_BLK_ALR = 33

NSA_Z = 5632
NSA_BLK_Q = 0
NSA_BLK_KC = 16
NSA_BLK_KS = 20
NSA_BLK_KW = 24
NSA_BLK_VC = 28
NSA_BLK_VS = 32
NSA_BLK_VW = 36
NSA_BLK_GATE = 40
NSA_ROPE_TILES = 7
NSA_Q_TILES = 4


def _cp(sem):
    return pltpu.CompilerParams(dimension_semantics=sem, vmem_limit_bytes=VMEM_LIMIT)


def _rms(x, w):
    return x * lax.rsqrt(jnp.mean(x * x, axis=-1, keepdims=True) + NORM_EPS) * w


def _split_cols(z, widths):
    out, off = [], 0
    for w in widths:
        out.append(z[..., off:off + w])
        off += w
    return out


def _tables_kernel(pos_ref, inv_ref, c2_ref, s2_ref, c128_ref, s128_ref):
    pos = pos_ref[...].astype(F32)
    lane = lax.broadcasted_iota(jnp.int32, (1, LANE), 1)
    lo = lane < 64
    a64 = pos * inv_ref[0:1, :]
    c2_ref[...] = jnp.where(lo, jnp.cos(a64), 0.0)
    s2_ref[...] = jnp.where(lo, jnp.sin(a64), 0.0)
    a128 = pos * inv_ref[1:2, :]
    s = jnp.sin(a128)
    c128_ref[...] = jnp.cos(a128)
    s128_ref[...] = jnp.where(lo, -s, s)


def rope_tables(positions):
    T = positions.size
    inv32 = jnp.power(ROPE_THETA, -jnp.arange(0, MLA_ROPE, 2, dtype=F32) / MLA_ROPE)
    inv64 = jnp.power(ROPE_THETA, -jnp.arange(0, NSA_HEAD_DIM, 2, dtype=F32) / NSA_HEAD_DIM)
    inv = jnp.zeros((8, LANE), F32)
    inv = inv.at[0, :64].set(jnp.concatenate([inv32, inv32]))
    inv = inv.at[1, :].set(jnp.concatenate([inv64, inv64]))
    tm = min(T, 1024)
    spec = pl.BlockSpec((tm, LANE), lambda i: (i, 0))
    return pl.pallas_call(
        _tables_kernel,
        out_shape=[jax.ShapeDtypeStruct((T, LANE), F32)] * 4,
        grid=(T // tm,),
        in_specs=[pl.BlockSpec((tm, 1), lambda i: (i, 0)), pl.BlockSpec((8, LANE), lambda i: (0, 0))],
        out_specs=[spec] * 4,
        compiler_params=_cp(("parallel",)),
        name="rope_tables",
    )(positions.reshape(T, 1), inv)


def _norm_mm_kernel(x_ref, nw_ref, w_ref, o_ref, xn_ref):
    @pl.when(pl.program_id(1) == 0)
    def _():
        xn_ref[...] = _rms(x_ref[...], nw_ref[...]).astype(BF16)

    o_ref[...] = jnp.dot(xn_ref[...], w_ref[...], preferred_element_type=F32).astype(o_ref.dtype)


def _norm_mm_rope_kernel(x_ref, nw_ref, w_ref, cos_ref, sin_ref, o_ref, xn_ref, *, tn, scale):
    j = pl.program_id(1)

    @pl.when(j == 0)
    def _():
        xn_ref[...] = _rms(x_ref[...], nw_ref[...]).astype(BF16)

    y = jnp.dot(xn_ref[...], w_ref[...], preferred_element_type=F32)

    def rope(mult):
        cos = cos_ref[...] * mult
        sin = sin_ref[...] * mult
        for c in range(tn // LANE):
            seg = y[:, c * LANE:(c + 1) * LANE]
            o_ref[:, c * LANE:(c + 1) * LANE] = (seg * cos + pltpu.roll(seg, 64, 1) * sin).astype(o_ref.dtype)

    @pl.when(j < NSA_Q_TILES)
    def _():
        rope(scale)

    @pl.when((j >= NSA_Q_TILES) & (j < NSA_ROPE_TILES))
    def _():
        rope(1.0)

    @pl.when(j >= NSA_ROPE_TILES)
    def _():
        o_ref[...] = y.astype(o_ref.dtype)


def _norm_swiglu_kernel(x_ref, nw_ref, wg_ref, wu_ref, o_ref, xn_ref):
    @pl.when(pl.program_id(1) == 0)
    def _():
        xn_ref[...] = _rms(x_ref[...], nw_ref[...]).astype(BF16)

    xn = xn_ref[...]
    g = jnp.dot(xn, wg_ref[...], preferred_element_type=F32)
    u = jnp.dot(xn, wu_ref[...], preferred_element_type=F32)
    o_ref[...] = (g * jax.nn.sigmoid(g) * u).astype(o_ref.dtype)


def _row_tile(T, want):
    return min(T, want)


def norm_matmul(x, nw, w, *, tm=1024, tn=512):
    T, D = x.shape
    N = w.shape[1]
    tm = _row_tile(T, tm)
    return pl.pallas_call(
        _norm_mm_kernel,
        out_shape=jax.ShapeDtypeStruct((T, N), BF16),
        grid=(T // tm, N // tn),
        in_specs=[pl.BlockSpec((tm, D), lambda i, j: (i, 0)),
                  pl.BlockSpec((1, D), lambda i, j: (0, 0)),
                  pl.BlockSpec((D, tn), lambda i, j: (0, j))],
        out_specs=pl.BlockSpec((tm, tn), lambda i, j: (i, j)),
        scratch_shapes=[pltpu.VMEM((tm, D), BF16)],
        compiler_params=_cp(("parallel", "arbitrary")),
        name="norm_matmul",
    )(x, nw.reshape(1, D), w)


def norm_matmul_rope(x, nw, w, cos, sin, *, tm=1024, tn=512):
    T, D = x.shape
    N = w.shape[1]
    tm = _row_tile(T, tm)
    kern = functools.partial(_norm_mm_rope_kernel, tn=tn, scale=NSA_HEAD_DIM ** -0.5)
    return pl.pallas_call(
        kern,
        out_shape=jax.ShapeDtypeStruct((T, N), BF16),
        grid=(T // tm, N // tn),
        in_specs=[pl.BlockSpec((tm, D), lambda i, j: (i, 0)),
                  pl.BlockSpec((1, D), lambda i, j: (0, 0)),
                  pl.BlockSpec((D, tn), lambda i, j: (0, j)),
                  pl.BlockSpec((tm, LANE), lambda i, j: (i, 0)),
                  pl.BlockSpec((tm, LANE), lambda i, j: (i, 0))],
        out_specs=pl.BlockSpec((tm, tn), lambda i, j: (i, j)),
        scratch_shapes=[pltpu.VMEM((tm, D), BF16)],
        compiler_params=_cp(("parallel", "arbitrary")),
        name="norm_matmul_rope",
    )(x, nw.reshape(1, D), w, cos, sin)


def norm_swiglu(x, nw, wg, wu, *, tm=1024, tn=512):
    T, D = x.shape
    N = wg.shape[1]
    tm = _row_tile(T, tm)
    return pl.pallas_call(
        _norm_swiglu_kernel,
        out_shape=jax.ShapeDtypeStruct((T, N), BF16),
        grid=(T // tm, N // tn),
        in_specs=[pl.BlockSpec((tm, D), lambda i, j: (i, 0)),
                  pl.BlockSpec((1, D), lambda i, j: (0, 0)),
                  pl.BlockSpec((D, tn), lambda i, j: (0, j)),
                  pl.BlockSpec((D, tn), lambda i, j: (0, j))],
        out_specs=pl.BlockSpec((tm, tn), lambda i, j: (i, j)),
        scratch_shapes=[pltpu.VMEM((tm, D), BF16)],
        compiler_params=_cp(("parallel", "arbitrary")),
        name="norm_swiglu",
    )(x, nw.reshape(1, D), wg, wu)


def _mm_norm_res_kernel(a_ref, w_ref, h_ref, nw_ref, o_ref, acc_ref):
    k = pl.program_id(1)

    @pl.when(k == 0)
    def _():
        acc_ref[...] = jnp.zeros_like(acc_ref)

    acc_ref[...] += jnp.dot(a_ref[...], w_ref[...], preferred_element_type=F32)

    @pl.when(k == pl.num_programs(1) - 1)
    def _():
        o_ref[...] = h_ref[...] + _rms(acc_ref[...], nw_ref[...])


def matmul_norm_residual(a, w, h, nw, *, tm=512, tk=512):
    T, K = a.shape
    D = w.shape[1]
    tm = _row_tile(T, tm)
    return pl.pallas_call(
        _mm_norm_res_kernel,
        out_shape=jax.ShapeDtypeStruct((T, D), F32),
        grid=(T // tm, K // tk),
        in_specs=[pl.BlockSpec((tm, tk), lambda i, k: (i, k)),
                  pl.BlockSpec((tk, D), lambda i, k: (k, 0)),
                  pl.BlockSpec((tm, D), lambda i, k: (i, 0)),
                  pl.BlockSpec((1, D), lambda i, k: (0, 0))],
        out_specs=pl.BlockSpec((tm, D), lambda i, k: (i, 0)),
        scratch_shapes=[pltpu.VMEM((tm, D), F32)],
        compiler_params=_cp(("parallel", "arbitrary")),
        name="matmul_norm_residual",
    )(a, w, h, nw.reshape(1, D))


def _ple_kernel(h_ref, wg_ref, bg_ref, p_ref, wp_ref, o_ref, hb_ref, *, tn):
    j = pl.program_id(1)

    @pl.when(j == 0)
    def _():
        hb_ref[...] = h_ref[...].astype(BF16)

    g = jnp.dot(hb_ref[...], wg_ref[...], preferred_element_type=F32) + bg_ref[...]
    pp = jnp.dot(p_ref[...].astype(BF16), wp_ref[...], preferred_element_type=F32)
    hs = h_ref[:, pl.ds(pl.multiple_of(j * tn, tn), tn)]
    o_ref[...] = hs + jax.nn.sigmoid(g) * pp


def ple(h, wg, bg, p, wp, *, tm=1024, tn=512):
    T, D = h.shape
    P = p.shape[1]
    tm = _row_tile(T, tm)
    return pl.pallas_call(
        functools.partial(_ple_kernel, tn=tn),
        out_shape=jax.ShapeDtypeStruct((T, D), F32),
        grid=(T // tm, D // tn),
        in_specs=[pl.BlockSpec((tm, D), lambda i, j: (i, 0)),
                  pl.BlockSpec((D, tn), lambda i, j: (0, j)),
                  pl.BlockSpec((1, tn), lambda i, j: (0, j)),
                  pl.BlockSpec((tm, P), lambda i, j: (i, 0)),
                  pl.BlockSpec((P, tn), lambda i, j: (0, j))],
        out_specs=pl.BlockSpec((tm, tn), lambda i, j: (i, j)),
        scratch_shapes=[pltpu.VMEM((tm, D), BF16)],
        compiler_params=_cp(("parallel", "arbitrary")),
        name="ple",
    )(h, wg, bg.reshape(1, D), p, wp)


def _split3(x):
    h1 = x.astype(BF16)
    r1 = x - h1.astype(F32)
    h2 = r1.astype(BF16)
    h3 = (r1 - h2.astype(F32)).astype(BF16)
    return h1, h2, h3


def _gla_kernel(qk_ref, v_ref, g_ref, alr_ref, wa_ref, ba_ref, nw_ref, o_ref, st_ref, *, tr):
    L = GLA_CHUNK

    @pl.when(pl.program_id(2) == 0)
    def _():
        st_ref[...] = jnp.zeros_like(st_ref)

    x = jnp.dot(alr_ref[0], wa_ref[0], preferred_element_type=F32) + ba_ref[0]
    log_a = (jnp.minimum(x, 0.0) - jnp.log1p(jnp.exp(-jnp.abs(x)))) * (1.0 / GLA_TAU)

    row = lax.broadcasted_iota(jnp.int32, (L, L), 0)
    col = lax.broadcasted_iota(jnp.int32, (L, L), 1)
    causal = row >= col
    tri = jnp.where(causal, 1.0, 0.0).astype(BF16)
    lo = lax.broadcasted_iota(jnp.int32, (1, LANE), 1) < GLA_DK
    nt = (((1,), (1,)), ((), ()))
    tn = (((0,), (0,)), ((), ()))

    for c in range(tr // L):
        rows = slice(c * L, (c + 1) * L)
        la = log_a[rows]
        a1, a2, a3 = _split3(la)
        b = (jnp.dot(tri, a1, preferred_element_type=F32) + jnp.dot(tri, a2, preferred_element_type=F32)
             + jnp.dot(tri, a3, preferred_element_type=F32))
        b_end = b[L - 1:L, :]
        blk = qk_ref[0, rows, :].astype(F32)
        swp = pltpu.roll(blk, 64, 1)
        q_dec = jnp.where(lo, blk * jnp.exp(b) * (GLA_DK ** -0.5), 0.0).astype(BF16)
        k_inv = jnp.where(lo, swp * jnp.exp(-b), 0.0).astype(BF16)
        k_end = jnp.where(lo, swp * jnp.exp(b_end - b), 0.0).astype(BF16)
        v = v_ref[0, rows, :]
        attn = lax.dot_general(q_dec, k_inv, nt, preferred_element_type=F32)
        attn = jnp.where(causal, attn, 0.0).astype(BF16)
        st = st_ref[...]
        o = (jnp.dot(attn, v, preferred_element_type=F32)
             + lax.dot_general(q_dec, st.astype(BF16), nt, preferred_element_type=F32))
        st_ref[...] = st * jnp.exp(b_end) + lax.dot_general(v, k_end, tn, preferred_element_type=F32)
        g = g_ref[0, rows, :].astype(F32)
        o_ref[0, rows, :] = (_rms(o, nw_ref[...]) * (g * jax.nn.sigmoid(g))).astype(o_ref.dtype)


def gla(z3, wa, ba, nw, *, tr=256):
    B, S, _ = z3.shape
    tr = min(tr, S)
    H = GLA_HEADS
    return pl.pallas_call(
        functools.partial(_gla_kernel, tr=tr),
        out_shape=jax.ShapeDtypeStruct((B, S, H * GLA_DV), BF16),
        grid=(B, H, S // tr),
        in_specs=[pl.BlockSpec((1, tr, LANE), lambda b, h, r: (b, r, AB_BLK_QK + h)),
                  pl.BlockSpec((1, tr, LANE), lambda b, h, r: (b, r, AB_BLK_V + h)),
                  pl.BlockSpec((1, tr, LANE), lambda b, h, r: (b, r, AB_BLK_G + h)),
                  pl.BlockSpec((1, tr, LANE), lambda b, h, r: (b, r, AB_BLK_ALR)),
                  pl.BlockSpec((1, LANE, LANE), lambda b, h, r: (h, 0, 0)),
                  pl.BlockSpec((1, 1, LANE), lambda b, h, r: (h, 0, 0)),
                  pl.BlockSpec((1, LANE), lambda b, h, r: (0, 0))],
        out_specs=pl.BlockSpec((1, tr, LANE), lambda b, h, r: (b, r, h)),
        scratch_shapes=[pltpu.VMEM((GLA_DV, LANE), F32)],
        compiler_params=_cp(("parallel", "parallel", "arbitrary")),
        name="gla",
    )(z3, z3, z3, z3, wa, ba, nw.reshape(1, GLA_DV))


def _mla_proj_kernel(cq_ref, ckv_ref, kr_ref, qnw_ref, kvnw_ref, wq_ref, wkv_ref, c2_ref, s2_ref,
                     q_ref, k_ref, v_ref, cqn_ref, ckvn_ref, kro_ref):
    @pl.when(pl.program_id(1) == 0)
    def _():
        cqn_ref[...] = _rms(cq_ref[...].astype(F32), qnw_ref[...]).astype(BF16)
        ckvn_ref[...] = _rms(ckv_ref[...].astype(F32), kvnw_ref[...]).astype(BF16)
        kr = kr_ref[...].astype(F32)
        kro_ref[...] = (kr * c2_ref[...] + pltpu.roll(kr, 64, 1) * s2_ref[...]).astype(BF16)

    scale = MLA_QK ** -0.5
    yq = jnp.dot(cqn_ref[...], wq_ref[0], preferred_element_type=F32)
    y2 = yq[:, LANE:]
    qr = y2 * c2_ref[...] + pltpu.roll(y2, 64, 1) * s2_ref[...]
    q_ref[0, 0, :, :MLA_NOPE] = (yq[:, :LANE] * scale).astype(BF16)
    q_ref[0, 0, :, MLA_NOPE:] = (qr[:, :MLA_ROPE] * scale).astype(BF16)
    ykv = jnp.dot(ckvn_ref[...], wkv_ref[0], preferred_element_type=F32)
    k_ref[0, 0, :, :MLA_NOPE] = ykv[:, :LANE].astype(BF16)
    k_ref[0, 0, :, MLA_NOPE:] = kro_ref[:, :MLA_ROPE]
    v_ref[0, 0] = ykv[:, LANE:].astype(BF16)


def mla_proj(z, qnw, kvnw, wq, wkv, c2, s2, B, S, *, tm=512):
    T = z.shape[0]
    tm = min(tm, S)
    H = MLA_HEADS
    nb = S // tm

    def omap(i, h):
        return (i // nb, h, i % nb, 0)

    return pl.pallas_call(
        _mla_proj_kernel,
        out_shape=[jax.ShapeDtypeStruct((B, H, S, MLA_QK), BF16),
                   jax.ShapeDtypeStruct((B, H, S, MLA_QK), BF16),
                   jax.ShapeDtypeStruct((B, H, S, MLA_V), BF16)],
        grid=(T // tm, H),
        in_specs=[pl.BlockSpec((tm, MLA_Q_RANK), lambda i, h: (i, AB_BLK_CQ // 4)),
                  pl.BlockSpec((tm, MLA_KV_RANK), lambda i, h: (i, AB_BLK_CKV // 4)),
                  pl.BlockSpec((tm, LANE), lambda i, h: (i, AB_BLK_KR)),
                  pl.BlockSpec((1, MLA_Q_RANK), lambda i, h: (0, 0)),
                  pl.BlockSpec((1, MLA_KV_RANK), lambda i, h: (0, 0)),
                  pl.BlockSpec((1, MLA_Q_RANK, 2 * LANE), lambda i, h: (h, 0, 0)),
                  pl.BlockSpec((1, MLA_KV_RANK, 2 * LANE), lambda i, h: (h, 0, 0)),
                  pl.BlockSpec((tm, LANE), lambda i, h: (i, 0)),
                  pl.BlockSpec((tm, LANE), lambda i, h: (i, 0))],
        out_specs=[pl.BlockSpec((1, 1, tm, MLA_QK), omap),
                   pl.BlockSpec((1, 1, tm, MLA_QK), omap),
                   pl.BlockSpec((1, 1, tm, MLA_V), omap)],
        scratch_shapes=[pltpu.VMEM((tm, MLA_Q_RANK), BF16), pltpu.VMEM((tm, MLA_KV_RANK), BF16),
                        pltpu.VMEM((tm, LANE), BF16)],
        compiler_params=_cp(("parallel", "arbitrary")),
        name="mla_proj",
    )(z, z, z, qnw.reshape(1, -1), kvnw.reshape(1, -1), wq, wkv, c2, s2)


def _flash_update(s, v, m_ref, l_ref, acc_ref):
    m_prev = m_ref[...]
    m_new = jnp.maximum(m_prev, jnp.max(s, axis=-1, keepdims=True))
    alpha = jnp.exp(m_prev - m_new)
    p = jnp.exp(s - m_new)
    l_ref[...] = alpha * l_ref[...] + jnp.sum(p, axis=-1, keepdims=True)
    acc_ref[...] = alpha * acc_ref[...] + jnp.dot(p.astype(BF16), v, preferred_element_type=F32)
    m_ref[...] = m_new


def _mla_attn_kernel(q_ref, k_ref, v_ref, o_ref, m_ref, l_ref, acc_ref, *, t):
    qi = pl.program_id(2)
    j = pl.program_id(3)

    @pl.when(j == 0)
    def _():
        m_ref[...] = jnp.full_like(m_ref, NEG)
        l_ref[...] = jnp.zeros_like(l_ref)
        acc_ref[...] = jnp.zeros_like(acc_ref)

    nt = (((1,), (1,)), ((), ()))

    @pl.when(j < qi)
    def _():
        s = lax.dot_general(q_ref[0, 0], k_ref[0, 0], nt, preferred_element_type=F32)
        _flash_update(s, v_ref[0, 0], m_ref, l_ref, acc_ref)

    @pl.when(j == qi)
    def _():
        s = lax.dot_general(q_ref[0, 0], k_ref[0, 0], nt, preferred_element_type=F32)
        row = lax.broadcasted_iota(jnp.int32, (t, t), 0)
        col = lax.broadcasted_iota(jnp.int32, (t, t), 1)
        s = jnp.where(col <= row, s, NEG)
        _flash_update(s, v_ref[0, 0], m_ref, l_ref, acc_ref)
        o_ref[0] = (acc_ref[...] / l_ref[...]).astype(o_ref.dtype)


def mla_attention(q, k, v, *, t=512):
    B, H, S, _ = q.shape
    t = min(t, S)
    n = S // t
    return pl.pallas_call(
        functools.partial(_mla_attn_kernel, t=t),
        out_shape=jax.ShapeDtypeStruct((B, S, H * MLA_V), BF16),
        grid=(B, H, n, n),
        in_specs=[pl.BlockSpec((1, 1, t, MLA_QK), lambda b, h, i, j: (b, h, i, 0)),
                  pl.BlockSpec((1, 1, t, MLA_QK), lambda b, h, i, j: (b, h, jnp.minimum(i, j), 0)),
                  pl.BlockSpec((1, 1, t, MLA_V), lambda b, h, i, j: (b, h, jnp.minimum(i, j), 0))],
        out_specs=pl.BlockSpec((1, t, MLA_V), lambda b, h, i, j: (b, i, h)),
        scratch_shapes=[pltpu.VMEM((t, 1), F32), pltpu.VMEM((t, 1), F32), pltpu.VMEM((t, MLA_V), F32)],
        compiler_params=_cp(("parallel", "parallel", "parallel", "arbitrary")),
        name="mla_attention",
    )(q, k, v)


def _nsa_compress_kernel(r_ref, pos_ref, w1_ref, w2_ref, o_ref):
    r = r_ref[0, 0, 0]
    half = r.shape[1]
    nr = r.shape[0]
    a = jnp.dot(r, w1_ref[0, :half, :], preferred_element_type=F32)
    b = jnp.dot(r, w1_ref[0, half:, :], preferred_element_type=F32)
    pos = jnp.broadcast_to(pos_ref[0], (8, 2 * half)).astype(BF16)
    c = jnp.dot(pos, w1_ref[0], preferred_element_type=F32)[0:1, :]
    pre = a + pltpu.roll(b, nr - 1, 0) + c
    o_ref[0, 0, 0] = jnp.dot(jax.nn.gelu(pre).astype(BF16), w2_ref[0],
                             preferred_element_type=F32).astype(o_ref.dtype)


def nsa_compress(r, pos, w1, w2):
    _, B, G, NR, W = r.shape
    dh = NSA_HEAD_DIM
    return pl.pallas_call(
        _nsa_compress_kernel,
        out_shape=jax.ShapeDtypeStruct((2, B, G, NR, dh), BF16),
        grid=(2, B, G),
        in_specs=[pl.BlockSpec((1, 1, 1, NR, W), lambda c, b, g: (c, b, g, 0, 0)),
                  pl.BlockSpec((1, 1, 2 * W), lambda c, b, g: (c, 0, 0)),
                  pl.BlockSpec((1, 2 * W, dh), lambda c, b, g: (c, 0, 0)),
                  pl.BlockSpec((1, dh, dh), lambda c, b, g: (c, 0, 0))],
        out_specs=pl.BlockSpec((1, 1, 1, NR, dh), lambda c, b, g: (c, b, g, 0, 0)),
        compiler_params=_cp(("parallel", "parallel", "parallel")),
        name="nsa_compress",
    )(r, pos, w1, w2)


def _nsa_cmp_kernel(q_ref, kc_ref, vc_ref, ovt_ref, o_ref, sel_ref, *, tq, ns, ksel):
    qi = pl.program_id(2)
    nc = kc_ref.shape[3]
    kc = kc_ref[0, 0, 0]
    vc = vc_ref[0, 0, 0]
    t = qi * tq + lax.broadcasted_iota(jnp.int32, (tq, 1), 0)
    n = lax.broadcasted_iota(jnp.int32, (1, nc), 1)
    ok = (n * NSA_CMP_STRIDE + (NSA_CMP_LEN - 1)) <= t
    nt = (((1,), (1,)), ((), ()))
    psum = jnp.zeros((tq, nc), F32)
    for j in range(NSA_HPG):
        q = q_ref[0, :, j * LANE:(j + 1) * LANE]
        s = lax.dot_general(q, kc, nt, preferred_element_type=F32)
        s = jnp.where(ok, s, NEG)
        e = jnp.where(ok, jnp.exp(s - jnp.max(s, axis=-1, keepdims=True)), 0.0)
        d = jnp.sum(e, axis=-1, keepdims=True)
        p = e * jnp.where(d > 0.0, 1.0 / d, 0.0)
        o_ref[0, :, j * LANE:(j + 1) * LANE] = jnp.dot(p.astype(BF16), vc,
                                                        preferred_element_type=F32).astype(o_ref.dtype)
        psum = psum + p
    ph = psum.astype(BF16)
    plo = (psum - ph.astype(F32)).astype(BF16)
    ovt = ovt_ref[...]
    imp = (lax.dot_general(ovt, ph, nt, preferred_element_type=F32)
           + lax.dot_general(ovt, plo, nt, preferred_element_type=F32))
    m = lax.broadcasted_iota(jnp.int32, (ns, 1), 0)
    tt = qi * tq + lax.broadcasted_iota(jnp.int32, (1, tq), 1)
    causal = m * NSA_SEL_LEN <= tt
    cur = lax.shift_right_logical(tt, 6)
    forced = (m == 0) | (m == cur) | (m == cur - 1)
    score = jnp.where(causal, jnp.where(forced, FORCE, imp), -FORCE)
    cnt = jnp.zeros((ns, tq), F32)
    for m2 in range(ns):
        r = score[m2:m2 + 1, :]
        ahead = (r > score) | ((r == score) & (m2 < m))
        cnt = cnt + jnp.where(ahead, 1.0, 0.0)
    sel = jnp.where((cnt < float(ksel)) & causal, 1.0, 0.0)
    selp = jnp.concatenate([sel, jnp.zeros((LANE - ns, tq), F32)], axis=0)
    sel_ref[0, 0] = selp.T.astype(sel_ref.dtype)


def nsa_cmp_select(z3, kv_cmp, ovt, *, tq=512):
    B, S, _ = z3.shape
    G = NSA_KV_GROUPS
    tq = min(tq, S)
    ns = S // NSA_SEL_LEN
    nc = kv_cmp.shape[3]
    kern = functools.partial(_nsa_cmp_kernel, tq=tq, ns=ns, ksel=min(NSA_SEL_TOPK, ns))
    return pl.pallas_call(
        kern,
        out_shape=[jax.ShapeDtypeStruct((B, S, NSA_HEADS * NSA_HEAD_DIM), BF16),
                   jax.ShapeDtypeStruct((B, G, S, LANE), BF16)],
        grid=(B, G, S // tq),
        in_specs=[pl.BlockSpec((1, tq, 4 * LANE), lambda b, g, i: (b, i, g)),
                  pl.BlockSpec((1, 1, 1, nc, LANE), lambda b, g, i: (0, b, g, 0, 0)),
                  pl.BlockSpec((1, 1, 1, nc, LANE), lambda b, g, i: (1, b, g, 0, 0)),
                  pl.BlockSpec((ns, nc), lambda b, g, i: (0, 0))],
        out_specs=[pl.BlockSpec((1, tq, 4 * LANE), lambda b, g, i: (b, i, g)),
                   pl.BlockSpec((1, 1, tq, LANE), lambda b, g, i: (b, g, i, 0))],
        compiler_params=_cp(("parallel", "parallel", "parallel")),
        name="nsa_cmp_select",
    )(z3, kv_cmp, kv_cmp, ovt)


def _nsa_sel_kernel(q_ref, k_ref, v_ref, sel_ref, o_ref, m_ref, l_ref, acc_ref, *, t):
    qi = pl.program_id(2)
    j = pl.program_id(3)

    @pl.when(j == 0)
    def _():
        m_ref[...] = jnp.full_like(m_ref, NEG)
        l_ref[...] = jnp.zeros_like(l_ref)
        acc_ref[...] = jnp.zeros_like(acc_ref)

    nt = (((1,), (1,)), ((), ()))

    def step(diag):
        blk = lax.broadcasted_iota(jnp.int32, (LANE, t), 0)
        key = j * t + lax.broadcasted_iota(jnp.int32, (LANE, t), 1)
        expand = jnp.where(lax.shift_right_logical(key, 6) == blk, 1.0, 0.0).astype(BF16)
        mask = jnp.dot(sel_ref[0, 0], expand, preferred_element_type=F32) > 0.5
        if diag:
            row = lax.broadcasted_iota(jnp.int32, (t, t), 0)
            col = lax.broadcasted_iota(jnp.int32, (t, t), 1)
            mask = mask & (col <= row)
        k = k_ref[0]
        v = v_ref[0]
        for h in range(NSA_HPG):
            s = lax.dot_general(q_ref[0, :, h * LANE:(h + 1) * LANE], k, nt, preferred_element_type=F32)
            s = jnp.where(mask, s, NEG)
            _flash_update(s, v, m_ref.at[h], l_ref.at[h], acc_ref.at[h])

    @pl.when(j < qi)
    def _():
        step(False)

    @pl.when(j == qi)
    def _():
        step(True)
        for h in range(NSA_HPG):
            o_ref[0, :, h * LANE:(h + 1) * LANE] = (acc_ref[h] / l_ref[h]).astype(o_ref.dtype)


def nsa_selected(z3, sel, *, t=256):
    B, S, _ = z3.shape
    G = NSA_KV_GROUPS
    t = min(t, S)
    n = S // t
    return pl.pallas_call(
        functools.partial(_nsa_sel_kernel, t=t),
        out_shape=jax.ShapeDtypeStruct((B, S, NSA_HEADS * NSA_HEAD_DIM), BF16),
        grid=(B, G, n, n),
        in_specs=[pl.BlockSpec((1, t, 4 * LANE), lambda b, g, i, j: (b, i, g)),
                  pl.BlockSpec((1, t, LANE), lambda b, g, i, j: (b, jnp.minimum(i, j), NSA_BLK_KS + g)),
                  pl.BlockSpec((1, t, LANE), lambda b, g, i, j: (b, jnp.minimum(i, j), NSA_BLK_VS + g)),
                  pl.BlockSpec((1, 1, t, LANE), lambda b, g, i, j: (b, g, i, 0))],
        out_specs=pl.BlockSpec((1, t, 4 * LANE), lambda b, g, i, j: (b, i, g)),
        scratch_shapes=[pltpu.VMEM((NSA_HPG, t, 1), F32), pltpu.VMEM((NSA_HPG, t, 1), F32),
                        pltpu.VMEM((NSA_HPG, t, LANE), F32)],
        compiler_params=_cp(("parallel", "parallel", "parallel", "arbitrary")),
        name="nsa_selected",
    )(z3, z3, z3, sel)


def _nsa_win_kernel(q_ref, k_ref, v_ref, oc_ref, os_ref, gl_ref, bg_ref, o_ref, m_ref, l_ref, acc_ref,
                    *, t, nw):
    qi = pl.program_id(2)
    w = pl.program_id(3)

    @pl.when(w == 0)
    def _():
        m_ref[...] = jnp.full_like(m_ref, NEG)
        l_ref[...] = jnp.zeros_like(l_ref)
        acc_ref[...] = jnp.zeros_like(acc_ref)

    nt = (((1,), (1,)), ((), ()))

    @pl.when(qi - w >= 0)
    def _():
        row = lax.broadcasted_iota(jnp.int32, (t, t), 0)
        col = lax.broadcasted_iota(jnp.int32, (t, t), 1) - w * t
        mask = (col <= row) & (col > row - NSA_WINDOW)
        k = k_ref[0]
        v = v_ref[0]
        for h in range(NSA_HPG):
            s = lax.dot_general(q_ref[0, :, h * LANE:(h + 1) * LANE], k, nt, preferred_element_type=F32)
            s = jnp.where(mask, s, NEG)
            _flash_update(s, v, m_ref.at[h], l_ref.at[h], acc_ref.at[h])

    @pl.when(w == nw - 1)
    def _():
        gates = jax.nn.sigmoid(gl_ref[0].astype(F32) + bg_ref[0])
        for h in range(NSA_HPG):
            cols = slice(h * LANE, (h + 1) * LANE)
            o_win = acc_ref[h] / l_ref[h]
            o = (gates[:, h:h + 1] * oc_ref[0, :, cols].astype(F32)
                 + gates[:, 4 + h:5 + h] * os_ref[0, :, cols].astype(F32)
                 + gates[:, 8 + h:9 + h] * o_win)
            o_ref[0, :, cols] = o.astype(o_ref.dtype)


def nsa_window_merge(z3, o_cmp, o_sel, bg, *, t=256):
    B, S, _ = z3.shape
    G = NSA_KV_GROUPS
    t = min(t, S)
    n = S // t
    nw = min(NSA_WINDOW // t + 1, n)

    def kv(blk):
        return lambda b, g, i, w: (b, jnp.maximum(i - w, 0), blk + g)

    big = pl.BlockSpec((1, t, 4 * LANE), lambda b, g, i, w: (b, i, g))
    return pl.pallas_call(
        functools.partial(_nsa_win_kernel, t=t, nw=nw),
        out_shape=jax.ShapeDtypeStruct((B, S, NSA_HEADS * NSA_HEAD_DIM), BF16),
        grid=(B, G, n, nw),
        in_specs=[big,
                  pl.BlockSpec((1, t, LANE), kv(NSA_BLK_KW)),
                  pl.BlockSpec((1, t, LANE), kv(NSA_BLK_VW)),
                  big, big,
                  pl.BlockSpec((1, t, LANE), lambda b, g, i, w: (b, i, NSA_BLK_GATE + g)),
                  pl.BlockSpec((1, 1, LANE), lambda b, g, i, w: (g, 0, 0))],
        out_specs=big,
        scratch_shapes=[pltpu.VMEM((NSA_HPG, t, 1), F32), pltpu.VMEM((NSA_HPG, t, 1), F32),
                        pltpu.VMEM((NSA_HPG, t, LANE), F32)],
        compiler_params=_cp(("parallel", "parallel", "parallel", "arbitrary")),
        name="nsa_window_merge",
    )(z3, z3, z3, o_cmp, o_sel, z3, bg)


def _rot_half_cols(w):
    half = w.shape[-1] // 2
    return jnp.concatenate([-w[..., half:], w[..., :half]], axis=-1)


def _prep_ab(w_in, w_alpha_up, b_alpha, w_uq, w_ukv):
    D = w_in.shape[0]
    q_g, k_g, v_g, g_g, a_lr, c_q, c_kv, k_r = _split_cols(w_in, AB_SPLITS)
    qk = jnp.concatenate([q_g.reshape(D, GLA_HEADS, GLA_DK), k_g.reshape(D, GLA_HEADS, GLA_DK)],
                         axis=-1).reshape(D, GLA_HEADS * LANE)
    tail = jnp.concatenate([k_r, _rot_half_cols(k_r), a_lr,
                            jnp.zeros((D, 512 - 2 * MLA_ROPE - GLA_GATE_RANK), w_in.dtype)], axis=-1)
    w = jnp.concatenate([qk, v_g, g_g, c_q, c_kv, tail], axis=-1).astype(BF16)
    wa = w_alpha_up.reshape(GLA_GATE_RANK, GLA_HEADS, GLA_DK).transpose(1, 0, 2)
    wa = jnp.concatenate([wa, wa], axis=-1)
    wa = jnp.pad(wa, ((0, 0), (0, LANE - GLA_GATE_RANK), (0, 0))).astype(BF16)
    ba = b_alpha.reshape(GLA_HEADS, 1, GLA_DK)
    ba = jnp.concatenate([ba, ba], axis=-1)
    wq = w_uq.reshape(MLA_Q_RANK, MLA_HEADS, MLA_QK)
    rope = wq[..., MLA_NOPE:]
    wq = jnp.concatenate([wq, _rot_half_cols(rope)], axis=-1).transpose(1, 0, 2).astype(BF16)
    wkv = w_ukv.reshape(MLA_KV_RANK, MLA_HEADS, MLA_NOPE + MLA_V).transpose(1, 0, 2).astype(BF16)
    return w, wa, ba, wq, wkv


def _prep_nsa(w_in, b_gate):
    D = w_in.shape[0]
    q, kc, vc, ks, vs, kw, vw, gl = _split_cols(w_in, NSA_SPLITS)
    G, HG = NSA_KV_GROUPS, NSA_HPG
    glp = gl.reshape(D, G, HG, 3).transpose(0, 1, 3, 2).reshape(D, G, 3 * HG)
    glp = jnp.pad(glp, ((0, 0), (0, 0), (0, LANE - 3 * HG))).reshape(D, G * LANE)
    w = jnp.concatenate([q, kc, ks, kw, vc, vs, vw, glp], axis=-1).astype(BF16)
    bg = b_gate.reshape(G, HG, 3).transpose(0, 2, 1).reshape(G, 1, 3 * HG)
    bg = jnp.pad(bg, ((0, 0), (0, 0), (0, LANE - 3 * HG)))
    return w, bg


def _overlap_t(S):
    nr = S // NSA_CMP_STRIDE
    ns = S // NSA_SEL_LEN
    c_start = np.arange(nr) * NSA_CMP_STRIDE
    c_end = c_start + NSA_CMP_LEN
    s_start = np.arange(ns) * NSA_SEL_LEN
    s_end = s_start + NSA_SEL_LEN
    ov = (c_start[None, :] < s_end[:, None]) & (c_end[None, :] > s_start[:, None])
    return jnp.asarray(ov.astype(np.float32), dtype=BF16)


def gla_mla_mixer(h, pre_w, B, S, c2, s2, w_in, w_alpha_up, b_alpha, gla_norm_w,
                  q_norm_w, w_uq, kv_norm_w, w_ukv):
    w, wa, ba, wq, wkv = _prep_ab(w_in, w_alpha_up, b_alpha, w_uq, w_ukv)
    z = norm_matmul(h, pre_w, w)
    z3 = z.reshape(B, S, AB_Z)
    o_gla = gla(z3, wa, ba, gla_norm_w)
    qm, km, vm = mla_proj(z, q_norm_w, kv_norm_w, wq, wkv, c2, s2, B, S)
    o_mla = mla_attention(qm, km, vm)
    return jnp.concatenate([o_gla, o_mla], axis=-1).reshape(B * S, -1)


def nsa_mixer(h, pre_w, B, S, c128, s128, w_in, b_gate, cmp_pos, cmp_w1, cmp_w2):
    w, bg = _prep_nsa(w_in, b_gate)
    z = norm_matmul_rope(h, pre_w, w, c128, s128)
    z3 = z.reshape(B, S, NSA_Z)
    G, dh = NSA_KV_GROUPS, NSA_HEAD_DIM

    def blocks(blk):
        t = z3[:, :, blk * LANE:(blk + G) * LANE].reshape(B, S, G, dh).transpose(0, 2, 1, 3)
        return t.reshape(B, G, S // NSA_CMP_STRIDE, NSA_CMP_STRIDE * dh)

    r = jnp.stack([blocks(NSA_BLK_KC), blocks(NSA_BLK_VC)])
    kv_cmp = nsa_compress(r, cmp_pos.reshape(2, 1, NSA_CMP_LEN * dh),
                          cmp_w1.astype(BF16), cmp_w2.astype(BF16))
    o_cmp, sel = nsa_cmp_select(z3, kv_cmp, _overlap_t(S))
    o_sel = nsa_selected(z3, sel)
    o = nsa_window_merge(z3, o_cmp, o_sel, bg)
    return o.reshape(B * S, -1)


def kernel(x, p, positions, ln_mix_pre, ln_mix_post, ln_ffn_pre, ln_ffn_post, ab_w_in, gla_w_alpha_up, gla_b_alpha, gla_norm_w, mla_q_norm_w, mla_w_uq, mla_kv_norm_w, mla_w_ukv, ab_w_out, nsa_w_in, nsa_b_gate, nsa_cmp_pos, nsa_cmp_w1, nsa_cmp_w2, nsa_w_out, ffn_w_gate, ffn_w_up, ffn_w_down, ple_w_gate, ple_b_gate, ple_w_proj):
    B, S, D = x.shape
    T = B * S
    depth = p.shape[0]
    c2, s2, c128, s128 = rope_tables(positions)
    h = x.reshape(T, D)
    for i in range(depth):
        j = i // 2
        if i % 2 == 0:
            mix = gla_mla_mixer(h, ln_mix_pre[i], B, S, c2, s2, ab_w_in[j], gla_w_alpha_up[j],
                                gla_b_alpha[j], gla_norm_w[j], mla_q_norm_w[j], mla_w_uq[j],
                                mla_kv_norm_w[j], mla_w_ukv[j])
            w_out = ab_w_out[j]
        else:
            mix = nsa_mixer(h, ln_mix_pre[i], B, S, c128, s128, nsa_w_in[j], nsa_b_gate[j],
                            nsa_cmp_pos[j], nsa_cmp_w1[j], nsa_cmp_w2[j])
            w_out = nsa_w_out[j]
        h = matmul_norm_residual(mix, w_out.astype(BF16), h, ln_mix_post[i])
        act = norm_swiglu(h, ln_ffn_pre[i], ffn_w_gate[i].astype(BF16), ffn_w_up[i].astype(BF16))
        h = matmul_norm_residual(act, ffn_w_down[i].astype(BF16), h, ln_ffn_post[i])
        h = ple(h, ple_w_gate[i].astype(BF16), ple_b_gate[i], p[i].reshape(T, -1),
                ple_w_proj[i].astype(BF16))
    return h.reshape(B, S, D)
```

```python
import functools

import numpy as np
import jax
import jax.numpy as jnp
from jax import lax
from jax.experimental import pallas as pl
from jax.experimental.pallas import tpu as pltpu

F32 = jnp.float32
BF16 = jnp.bfloat16

D_MODEL = 2048
PLE_DIM = 256
ROPE_THETA = 10000.0
NORM_EPS = 1e-6
NEG = -1e30
FORCE = 1e6

GLA_HEADS = 8
GLA_DK = 64
GLA_DV = 128
GLA_GATE_RANK = 16
GLA_TAU = 16.0
GLA_CHUNK = 64

MLA_HEADS = 8
MLA_Q_RANK = 512
MLA_KV_RANK = 512
MLA_NOPE = 128
MLA_ROPE = 64
MLA_V = 128
MLA_QK = MLA_NOPE + MLA_ROPE

NSA_HEADS = 16
NSA_KV_GROUPS = 4
NSA_HPG = NSA_HEADS // NSA_KV_GROUPS
NSA_HEAD_DIM = 128
NSA_CMP_LEN = 32
NSA_CMP_STRIDE = 16
NSA_SEL_LEN = 64
NSA_SEL_TOPK = 16
NSA_WINDOW = 512

FFN_HIDDEN = 5632

AB_SPLITS = (GLA_HEADS * GLA_DK, GLA_HEADS * GLA_DK, GLA_HEADS * GLA_DV, GLA_HEADS * GLA_DV,
             GLA_GATE_RANK, MLA_Q_RANK, MLA_KV_RANK, MLA_ROPE)
NSA_KV_W = NSA_KV_GROUPS * NSA_HEAD_DIM
NSA_SPLITS = (NSA_HEADS * NSA_HEAD_DIM,) + (NSA_KV_W,) * 6 + (3 * NSA_HEADS,)

LOG2E = 1.4426950408889634
LANE = 128
VMEM_LIMIT = 56 * 1024 * 1024

AB_Z = 4608
AB_BLK_QK = 0
AB_BLK_V = 8
AB_BLK_G = 16
AB_BLK_CQ = 24
AB_BLK_CKV = 28
AB_BLK_KR = 32
AB_BLK_ALR = 33

NSA_Z = 5632
NSA_BLK_Q = 0
NSA_BLK_KC = 16
NSA_BLK_KS = 20
NSA_BLK_KW = 24
NSA_BLK_VC = 28
NSA_BLK_VS = 32
NSA_BLK_VW = 36
NSA_BLK_GATE = 40
NSA_ROPE_TILES = 7
NSA_Q_TILES = 4


def _cp(sem):
    return pltpu.CompilerParams(dimension_semantics=sem, vmem_limit_bytes=VMEM_LIMIT)


def _rms(x, w):
    return x * lax.rsqrt(jnp.mean(x * x, axis=-1, keepdims=True) + NORM_EPS) * w


def _split_cols(z, widths):
    out, off = [], 0
    for w in widths:
        out.append(z[..., off:off + w])
        off += w
    return out


def _tables_kernel(pos_ref, inv_ref, c2_ref, s2_ref, c128_ref, s128_ref):
    pos = pos_ref[...].astype(F32)
    lane = lax.broadcasted_iota(jnp.int32, (1, LANE), 1)
    lo = lane < 64
    a64 = pos * inv_ref[0:1, :]
    c2_ref[...] = jnp.where(lo, jnp.cos(a64), 0.0)
    s2_ref[...] = jnp.where(lo, jnp.sin(a64), 0.0)
    a128 = pos * inv_ref[1:2, :]
    s = jnp.sin(a128)
    c128_ref[...] = jnp.cos(a128)
    s128_ref[...] = jnp.where(lo, -s, s)


def rope_tables(positions):
    T = positions.size
    inv32 = jnp.power(ROPE_THETA, -jnp.arange(0, MLA_ROPE, 2, dtype=F32) / MLA_ROPE)
    inv64 = jnp.power(ROPE_THETA, -jnp.arange(0, NSA_HEAD_DIM, 2, dtype=F32) / NSA_HEAD_DIM)
    inv = jnp.zeros((8, LANE), F32)
    inv = inv.at[0, :64].set(jnp.concatenate([inv32, inv32]))
    inv = inv.at[1, :].set(jnp.concatenate([inv64, inv64]))
    tm = min(T, 1024)
    spec = pl.BlockSpec((tm, LANE), lambda i: (i, 0))
    return pl.pallas_call(
        _tables_kernel,
        out_shape=[jax.ShapeDtypeStruct((T, LANE), F32)] * 4,
        grid=(T // tm,),
        in_specs=[pl.BlockSpec((tm, 1), lambda i: (i, 0)), pl.BlockSpec((8, LANE), lambda i: (0, 0))],
        out_specs=[spec] * 4,
        compiler_params=_cp(("parallel",)),
        name="rope_tables",
    )(positions.reshape(T, 1), inv)


def _norm_mm_kernel(x_ref, nw_ref, w_ref, o_ref, xn_ref):
    @pl.when(pl.program_id(1) == 0)
    def _():
        xn_ref[...] = _rms(x_ref[...], nw_ref[...]).astype(BF16)

    o_ref[...] = jnp.dot(xn_ref[...], w_ref[...], preferred_element_type=F32).astype(o_ref.dtype)


def _norm_mm_rope_kernel(x_ref, nw_ref, w_ref, cos_ref, sin_ref, o_ref, xn_ref, *, tn, scale):
    j = pl.program_id(1)

    @pl.when(j == 0)
    def _():
        xn_ref[...] = _rms(x_ref[...], nw_ref[...]).astype(BF16)

    y = jnp.dot(xn_ref[...], w_ref[...], preferred_element_type=F32)
    mult = jnp.where(j < NSA_Q_TILES, scale, 1.0)
    is_rope = j < NSA_ROPE_TILES
    cos = jnp.where(is_rope, cos_ref[...] * mult, 1.0)
    sin = jnp.where(is_rope, sin_ref[...] * mult, 0.0)
    for c in range(tn // LANE):
        seg = y[:, c * LANE:(c + 1) * LANE]
        o_ref[:, c * LANE:(c + 1) * LANE] = (seg * cos + pltpu.roll(seg, 64, 1) * sin).astype(o_ref.dtype)


def _norm_swiglu_kernel(x_ref, nw_ref, wg_ref, wu_ref, o_ref, xn_ref):
    @pl.when(pl.program_id(1) == 0)
    def _():
        xn_ref[...] = _rms(x_ref[...], nw_ref[...]).astype(BF16)

    xn = xn_ref[...]
    g = jnp.dot(xn, wg_ref[...], preferred_element_type=F32)
    u = jnp.dot(xn, wu_ref[...], preferred_element_type=F32)
    o_ref[...] = (g * jax.nn.sigmoid(g) * u).astype(o_ref.dtype)


def _row_tile(T, want):
    return min(T, want)


def norm_matmul(x, nw, w, *, tm=1024, tn=512):
    T, D = x.shape
    N = w.shape[1]
    tm = _row_tile(T, tm)
    return pl.pallas_call(
        _norm_mm_kernel,
        out_shape=jax.ShapeDtypeStruct((T, N), BF16),
        grid=(T // tm, N // tn),
        in_specs=[pl.BlockSpec((tm, D), lambda i, j: (i, 0)),
                  pl.BlockSpec((1, D), lambda i, j: (0, 0)),
                  pl.BlockSpec((D, tn), lambda i, j: (0, j))],
        out_specs=pl.BlockSpec((tm, tn), lambda i, j: (i, j)),
        scratch_shapes=[pltpu.VMEM((tm, D), BF16)],
        compiler_params=_cp(("parallel", "arbitrary")),
        name="norm_matmul",
    )(x, nw.reshape(1, D), w)


def norm_matmul_rope(x, nw, w, cos, sin, *, tm=1024, tn=512):
    T, D = x.shape
    N = w.shape[1]
    tm = _row_tile(T, tm)
    kern = functools.partial(_norm_mm_rope_kernel, tn=tn, scale=NSA_HEAD_DIM ** -0.5 * LOG2E)
    return pl.pallas_call(
        kern,
        out_shape=jax.ShapeDtypeStruct((T, N), BF16),
        grid=(T // tm, N // tn),
        in_specs=[pl.BlockSpec((tm, D), lambda i, j: (i, 0)),
                  pl.BlockSpec((1, D), lambda i, j: (0, 0)),
                  pl.BlockSpec((D, tn), lambda i, j: (0, j)),
                  pl.BlockSpec((tm, LANE), lambda i, j: (i, 0)),
                  pl.BlockSpec((tm, LANE), lambda i, j: (i, 0))],
        out_specs=pl.BlockSpec((tm, tn), lambda i, j: (i, j)),
        scratch_shapes=[pltpu.VMEM((tm, D), BF16)],
        compiler_params=_cp(("parallel", "arbitrary")),
        name="norm_matmul_rope",
    )(x, nw.reshape(1, D), w, cos, sin)


def norm_swiglu(x, nw, wg, wu, *, tm=1024, tn=512):
    T, D = x.shape
    N = wg.shape[1]
    tm = _row_tile(T, tm)
    return pl.pallas_call(
        _norm_swiglu_kernel,
        out_shape=jax.ShapeDtypeStruct((T, N), BF16),
        grid=(T // tm, N // tn),
        in_specs=[pl.BlockSpec((tm, D), lambda i, j: (i, 0)),
                  pl.BlockSpec((1, D), lambda i, j: (0, 0)),
                  pl.BlockSpec((D, tn), lambda i, j: (0, j)),
                  pl.BlockSpec((D, tn), lambda i, j: (0, j))],
        out_specs=pl.BlockSpec((tm, tn), lambda i, j: (i, j)),
        scratch_shapes=[pltpu.VMEM((tm, D), BF16)],
        compiler_params=_cp(("parallel", "arbitrary")),
        name="norm_swiglu",
    )(x, nw.reshape(1, D), wg, wu)


def _mm_norm_res_kernel(a_ref, w_ref, h_ref, nw_ref, o_ref, acc_ref):
    k = pl.program_id(1)

    @pl.when(k == 0)
    def _():
        acc_ref[...] = jnp.zeros_like(acc_ref)

    acc_ref[...] += jnp.dot(a_ref[...], w_ref[...], preferred_element_type=F32)

    @pl.when(k == pl.num_programs(1) - 1)
    def _():
        o_ref[...] = h_ref[...] + _rms(acc_ref[...], nw_ref[...])


def _mm_norm_res_1k_kernel(a_ref, w_ref, h_ref, nw_ref, o_ref):
    m = jnp.dot(a_ref[...], w_ref[...], preferred_element_type=F32)
    o_ref[...] = h_ref[...] + _rms(m, nw_ref[...])


def matmul_norm_residual(a, w, h, nw, *, tm=512, tk=512):
    T, K = a.shape
    D = w.shape[1]
    tm = _row_tile(T, tm)
    if tk == K:
        return pl.pallas_call(
            _mm_norm_res_1k_kernel,
            out_shape=jax.ShapeDtypeStruct((T, D), F32),
            grid=(T // tm,),
            in_specs=[pl.BlockSpec((tm, K), lambda i: (i, 0)),
                      pl.BlockSpec((K, D), lambda i: (0, 0)),
                      pl.BlockSpec((tm, D), lambda i: (i, 0)),
                      pl.BlockSpec((1, D), lambda i: (0, 0))],
            out_specs=pl.BlockSpec((tm, D), lambda i: (i, 0)),
            compiler_params=_cp(("parallel",)),
            name="matmul_norm_residual_1k",
        )(a, w, h, nw.reshape(1, D))
    return pl.pallas_call(
        _mm_norm_res_kernel,
        out_shape=jax.ShapeDtypeStruct((T, D), F32),
        grid=(T // tm, K // tk),
        in_specs=[pl.BlockSpec((tm, tk), lambda i, k: (i, k)),
                  pl.BlockSpec((tk, D), lambda i, k: (k, 0)),
                  pl.BlockSpec((tm, D), lambda i, k: (i, 0)),
                  pl.BlockSpec((1, D), lambda i, k: (0, 0))],
        out_specs=pl.BlockSpec((tm, D), lambda i, k: (i, 0)),
        scratch_shapes=[pltpu.VMEM((tm, D), F32)],
        compiler_params=_cp(("parallel", "arbitrary")),
        name="matmul_norm_residual",
    )(a, w, h, nw.reshape(1, D))


def _ple_kernel(h_ref, wg_ref, bg_ref, p_ref, wp_ref, o_ref, hb_ref, *, tn):
    j = pl.program_id(1)

    @pl.when(j == 0)
    def _():
        hb_ref[...] = h_ref[...].astype(BF16)

    g = jnp.dot(hb_ref[...], wg_ref[...], preferred_element_type=F32) + bg_ref[...]
    pp = jnp.dot(p_ref[...].astype(BF16), wp_ref[...], preferred_element_type=F32)
    hs = h_ref[:, pl.ds(pl.multiple_of(j * tn, tn), tn)]
    o_ref[...] = hs + jax.nn.sigmoid(g) * pp


def ple(h, wg, bg, p, wp, *, tm=1024, tn=512):
    T, D = h.shape
    P = p.shape[1]
    tm = _row_tile(T, tm)
    return pl.pallas_call(
        functools.partial(_ple_kernel, tn=tn),
        out_shape=jax.ShapeDtypeStruct((T, D), F32),
        grid=(T // tm, D // tn),
        in_specs=[pl.BlockSpec((tm, D), lambda i, j: (i, 0)),
                  pl.BlockSpec((D, tn), lambda i, j: (0, j)),
                  pl.BlockSpec((1, tn), lambda i, j: (0, j)),
                  pl.BlockSpec((tm, P), lambda i, j: (i, 0)),
                  pl.BlockSpec((P, tn), lambda i, j: (0, j))],
        out_specs=pl.BlockSpec((tm, tn), lambda i, j: (i, j)),
        scratch_shapes=[pltpu.VMEM((tm, D), BF16)],
        compiler_params=_cp(("parallel", "arbitrary")),
        name="ple",
    )(h, wg, bg.reshape(1, D), p, wp)


def _split3(x):
    h1 = x.astype(BF16)
    r1 = x - h1.astype(F32)
    h2 = r1.astype(BF16)
    h3 = (r1 - h2.astype(F32)).astype(BF16)
    return h1, h2, h3


def _gla_kernel(qk_ref, v_ref, g_ref, alr_ref, wa_ref, ba_ref, nw_ref, o_ref, st_ref, *, tr):
    L = GLA_CHUNK

    @pl.when(pl.program_id(2) == 0)
    def _():
        st_ref[...] = jnp.zeros_like(st_ref)

    x = jnp.dot(alr_ref[0], wa_ref[0], preferred_element_type=F32) + ba_ref[0]
    log_a = (jnp.minimum(x, 0.0) - jnp.log1p(jnp.exp(-jnp.abs(x)))) * (1.0 / GLA_TAU)

    row = lax.broadcasted_iota(jnp.int32, (L, L), 0)
    col = lax.broadcasted_iota(jnp.int32, (L, L), 1)
    causal = row >= col
    tri = jnp.where(causal, 1.0, 0.0).astype(BF16)
    lo = lax.broadcasted_iota(jnp.int32, (1, LANE), 1) < GLA_DK
    nt = (((1,), (1,)), ((), ()))
    tn = (((0,), (0,)), ((), ()))

    for c in range(tr // L):
        rows = slice(c * L, (c + 1) * L)
        la = log_a[rows]
        a1, a2, a3 = _split3(la)
        b = (jnp.dot(tri, a1, preferred_element_type=F32) + jnp.dot(tri, a2, preferred_element_type=F32)
             + jnp.dot(tri, a3, preferred_element_type=F32))
        b_end = b[L - 1:L, :]
        blk = qk_ref[0, rows, :].astype(F32)
        swp = pltpu.roll(blk, 64, 1)
        q_dec = jnp.where(lo, blk * jnp.exp(b) * (GLA_DK ** -0.5), 0.0).astype(BF16)
        k_inv = jnp.where(lo, swp * jnp.exp(-b), 0.0).astype(BF16)
        k_end = jnp.where(lo, swp * jnp.exp(b_end - b), 0.0).astype(BF16)
        v = v_ref[0, rows, :]
        attn = lax.dot_general(q_dec, k_inv, nt, preferred_element_type=F32)
        attn = jnp.where(causal, attn, 0.0).astype(BF16)
        st = st_ref[...]
        o = (jnp.dot(attn, v, preferred_element_type=F32)
             + lax.dot_general(q_dec, st.astype(BF16), nt, preferred_element_type=F32))
        st_ref[...] = st * jnp.exp(b_end) + lax.dot_general(v, k_end, tn, preferred_element_type=F32)
        g = g_ref[0, rows, :].astype(F32)
        o_ref[0, rows, :] = (_rms(o, nw_ref[...]) * (g * jax.nn.sigmoid(g))).astype(o_ref.dtype)


def gla(z3, wa, ba, nw, *, tr=256):
    B, S, _ = z3.shape
    tr = min(tr, S)
    H = GLA_HEADS
    return pl.pallas_call(
        functools.partial(_gla_kernel, tr=tr),
        out_shape=jax.ShapeDtypeStruct((B, S, H * GLA_DV), BF16),
        grid=(B, H, S // tr),
        in_specs=[pl.BlockSpec((1, tr, LANE), lambda b, h, r: (b, r, AB_BLK_QK + h)),
                  pl.BlockSpec((1, tr, LANE), lambda b, h, r: (b, r, AB_BLK_V + h)),
                  pl.BlockSpec((1, tr, LANE), lambda b, h, r: (b, r, AB_BLK_G + h)),
                  pl.BlockSpec((1, tr, LANE), lambda b, h, r: (b, r, AB_BLK_ALR)),
                  pl.BlockSpec((1, LANE, LANE), lambda b, h, r: (h, 0, 0)),
                  pl.BlockSpec((1, 1, LANE), lambda b, h, r: (h, 0, 0)),
                  pl.BlockSpec((1, LANE), lambda b, h, r: (0, 0))],
        out_specs=pl.BlockSpec((1, tr, LANE), lambda b, h, r: (b, r, h)),
        scratch_shapes=[pltpu.VMEM((GLA_DV, LANE), F32)],
        compiler_params=_cp(("parallel", "parallel", "arbitrary")),
        name="gla",
    )(z3, z3, z3, z3, wa, ba, nw.reshape(1, GLA_DV))


def _mla_proj_kernel(cq_ref, ckv_ref, kr_ref, qnw_ref, kvnw_ref, wq_ref, wkv_ref, c2_ref, s2_ref,
                     q_ref, k_ref, v_ref, cqn_ref, ckvn_ref, kro_ref):
    @pl.when(pl.program_id(1) == 0)
    def _():
        cqn_ref[...] = _rms(cq_ref[...].astype(F32), qnw_ref[...]).astype(BF16)
        ckvn_ref[...] = _rms(ckv_ref[...].astype(F32), kvnw_ref[...]).astype(BF16)
        kr = kr_ref[...].astype(F32)
        kro_ref[...] = (kr * c2_ref[...] + pltpu.roll(kr, 64, 1) * s2_ref[...]).astype(BF16)

    scale = MLA_QK ** -0.5 * LOG2E
    yq = jnp.dot(cqn_ref[...], wq_ref[0], preferred_element_type=F32)
    y2 = yq[:, LANE:]
    qr = y2 * c2_ref[...] + pltpu.roll(y2, 64, 1) * s2_ref[...]
    q_ref[0, 0, :, :MLA_NOPE] = (yq[:, :LANE] * scale).astype(BF16)
    q_ref[0, 0, :, MLA_NOPE:] = (qr[:, :MLA_ROPE] * scale).astype(BF16)
    ykv = jnp.dot(ckvn_ref[...], wkv_ref[0], preferred_element_type=F32)
    k_ref[0, 0, :, :MLA_NOPE] = ykv[:, :LANE].astype(BF16)
    k_ref[0, 0, :, MLA_NOPE:] = kro_ref[:, :MLA_ROPE]
    v_ref[0, 0] = ykv[:, LANE:].astype(BF16)


def mla_proj(z, qnw, kvnw, wq, wkv, c2, s2, B, S, *, tm=512):
    T = z.shape[0]
    tm = min(tm, S)
    H = MLA_HEADS
    nb = S // tm

    def omap(i, h):
        return (i // nb, h, i % nb, 0)

    return pl.pallas_call(
        _mla_proj_kernel,
        out_shape=[jax.ShapeDtypeStruct((B, H, S, MLA_QK), BF16),
                   jax.ShapeDtypeStruct((B, H, S, MLA_QK), BF16),
                   jax.ShapeDtypeStruct((B, H, S, MLA_V), BF16)],
        grid=(T // tm, H),
        in_specs=[pl.BlockSpec((tm, MLA_Q_RANK), lambda i, h: (i, AB_BLK_CQ // 4)),
                  pl.BlockSpec((tm, MLA_KV_RANK), lambda i, h: (i, AB_BLK_CKV // 4)),
                  pl.BlockSpec((tm, LANE), lambda i, h: (i, AB_BLK_KR)),
                  pl.BlockSpec((1, MLA_Q_RANK), lambda i, h: (0, 0)),
                  pl.BlockSpec((1, MLA_KV_RANK), lambda i, h: (0, 0)),
                  pl.BlockSpec((1, MLA_Q_RANK, 2 * LANE), lambda i, h: (h, 0, 0)),
                  pl.BlockSpec((1, MLA_KV_RANK, 2 * LANE), lambda i, h: (h, 0, 0)),
                  pl.BlockSpec((tm, LANE), lambda i, h: (i, 0)),
                  pl.BlockSpec((tm, LANE), lambda i, h: (i, 0))],
        out_specs=[pl.BlockSpec((1, 1, tm, MLA_QK), omap),
                   pl.BlockSpec((1, 1, tm, MLA_QK), omap),
                   pl.BlockSpec((1, 1, tm, MLA_V), omap)],
        scratch_shapes=[pltpu.VMEM((tm, MLA_Q_RANK), BF16), pltpu.VMEM((tm, MLA_KV_RANK), BF16),
                        pltpu.VMEM((tm, LANE), BF16)],
        compiler_params=_cp(("parallel", "arbitrary")),
        name="mla_proj",
    )(z, z, z, qnw.reshape(1, -1), kvnw.reshape(1, -1), wq, wkv, c2, s2)


_NT = (((1,), (1,)), ((), ()))


def _softmax_tile(s, v, m_ref, l_ref, acc_ref, first):
    M, tk = s.shape
    chunks = [s[:, c * LANE:(c + 1) * LANE] for c in range(tk // LANE)]
    mrow = jnp.max(functools.reduce(jnp.maximum, chunks), axis=-1, keepdims=True)
    if first:
        m_new = jnp.broadcast_to(mrow, (M, LANE))
    else:
        m_prev = m_ref[...]
        m_new = jnp.maximum(m_prev, mrow)
    ps = [jnp.exp2(c - m_new) for c in chunks]
    lsum = functools.reduce(jnp.add, ps)
    p = (jnp.concatenate(ps, axis=1) if len(ps) > 1 else ps[0]).astype(BF16)
    pv = jnp.dot(p, v, preferred_element_type=F32)
    if first:
        l_ref[...] = lsum
        acc_ref[...] = pv
    else:
        alpha = jnp.exp2(m_prev - m_new)
        l_ref[...] = alpha * l_ref[...] + lsum
        acc_ref[...] = alpha * acc_ref[...] + pv
    m_ref[...] = m_new


def _softmax_finish(l_ref, acc_ref):
    return acc_ref[...] / jnp.sum(l_ref[...], axis=-1, keepdims=True)


def _mla_attn_kernel(q_ref, k_ref, v_ref, o_ref, m_ref, l_ref, acc_ref, *, tq, tk):
    qi = pl.program_id(2)
    r = tq // tk
    q = q_ref[0, 0]
    row = lax.broadcasted_iota(jnp.int32, (tq, tk), 0)
    col = lax.broadcasted_iota(jnp.int32, (tq, tk), 1)

    def tile(j, d, first):
        off = pl.multiple_of(j * tk, tk)
        s = lax.dot_general(q, k_ref[0, 0, pl.ds(off, tk), :], _NT, preferred_element_type=F32)
        if d is not None:
            s = jnp.where(col + d * tk <= row, s, NEG)
        _softmax_tile(s, v_ref[0, 0, pl.ds(off, tk), :], m_ref, l_ref, acc_ref, first)

    for d in range(r):
        tile(qi * r + d, d, d == 0)

    def body(j, carry):
        tile(j, None, False)
        return carry

    lax.fori_loop(0, qi * r, body, 0)
    o_ref[0] = _softmax_finish(l_ref, acc_ref).astype(o_ref.dtype)


def mla_attention(q, k, v, *, tq=512, tk=256):
    B, H, S, _ = q.shape
    tq = min(tq, S)
    tk = min(tk, tq)
    return pl.pallas_call(
        functools.partial(_mla_attn_kernel, tq=tq, tk=tk),
        out_shape=jax.ShapeDtypeStruct((B, S, H * MLA_V), BF16),
        grid=(B, H, S // tq),
        in_specs=[pl.BlockSpec((1, 1, tq, MLA_QK), lambda b, h, i: (b, h, i, 0)),
                  pl.BlockSpec((1, 1, S, MLA_QK), lambda b, h, i: (b, h, 0, 0)),
                  pl.BlockSpec((1, 1, S, MLA_V), lambda b, h, i: (b, h, 0, 0))],
        out_specs=pl.BlockSpec((1, tq, MLA_V), lambda b, h, i: (b, i, h)),
        scratch_shapes=[pltpu.VMEM((tq, LANE), F32), pltpu.VMEM((tq, LANE), F32),
                        pltpu.VMEM((tq, MLA_V), F32)],
        compiler_params=_cp(("parallel", "parallel", "arbitrary")),
        name="mla_attention",
    )(q, k, v)


def _nsa_compress_kernel(r_ref, pos_ref, w1_ref, w2_ref, o_ref):
    r = r_ref[0, 0, 0]
    half = r.shape[1]
    nr = r.shape[0]
    a = jnp.dot(r, w1_ref[0, :half, :], preferred_element_type=F32)
    b = jnp.dot(r, w1_ref[0, half:, :], preferred_element_type=F32)
    pos = jnp.broadcast_to(pos_ref[0], (8, 2 * half)).astype(BF16)
    c = jnp.dot(pos, w1_ref[0], preferred_element_type=F32)[0:1, :]
    pre = a + pltpu.roll(b, nr - 1, 0) + c
    o_ref[0, 0, 0] = jnp.dot(jax.nn.gelu(pre).astype(BF16), w2_ref[0],
                             preferred_element_type=F32).astype(o_ref.dtype)


def nsa_compress(r, pos, w1, w2):
    _, B, G, NR, W = r.shape
    dh = NSA_HEAD_DIM
    return pl.pallas_call(
        _nsa_compress_kernel,
        out_shape=jax.ShapeDtypeStruct((2, B, G, NR, dh), BF16),
        grid=(2, B, G),
        in_specs=[pl.BlockSpec((1, 1, 1, NR, W), lambda c, b, g: (c, b, g, 0, 0)),
                  pl.BlockSpec((1, 1, 2 * W), lambda c, b, g: (c, 0, 0)),
                  pl.BlockSpec((1, 2 * W, dh), lambda c, b, g: (c, 0, 0)),
                  pl.BlockSpec((1, dh, dh), lambda c, b, g: (c, 0, 0))],
        out_specs=pl.BlockSpec((1, 1, 1, NR, dh), lambda c, b, g: (c, b, g, 0, 0)),
        compiler_params=_cp(("parallel", "parallel", "parallel")),
        name="nsa_compress",
    )(r, pos, w1, w2)


def _nsa_cmp_kernel(q_ref, kc_ref, vc_ref, ovt_ref, o_ref, sel_ref, *, tq, ns, ksel):
    qi = pl.program_id(2)
    nc = kc_ref.shape[3]
    kc = kc_ref[0, 0, 0]
    vc = vc_ref[0, 0, 0]
    t = qi * tq + lax.broadcasted_iota(jnp.int32, (tq, 1), 0)
    n = lax.broadcasted_iota(jnp.int32, (1, nc), 1)
    ok = (n * NSA_CMP_STRIDE + (NSA_CMP_LEN - 1)) <= t
    nt = (((1,), (1,)), ((), ()))
    psum = jnp.zeros((tq, nc), F32)
    for j in range(NSA_HPG):
        q = q_ref[0, :, j * LANE:(j + 1) * LANE]
        s = lax.dot_general(q, kc, nt, preferred_element_type=F32)
        s = jnp.where(ok, s, NEG)
        e = jnp.where(ok, jnp.exp2(s - jnp.max(s, axis=-1, keepdims=True)), 0.0)
        d = jnp.sum(e, axis=-1, keepdims=True)
        p = e * jnp.where(d > 0.0, 1.0 / d, 0.0)
        o_ref[0, :, j * LANE:(j + 1) * LANE] = jnp.dot(p.astype(BF16), vc,
                                                        preferred_element_type=F32).astype(o_ref.dtype)
        psum = psum + p
    ph = psum.astype(BF16)
    plo = (psum - ph.astype(F32)).astype(BF16)
    ovt = ovt_ref[...]
    imp = (lax.dot_general(ovt, ph, nt, preferred_element_type=F32)
           + lax.dot_general(ovt, plo, nt, preferred_element_type=F32))
    m = lax.broadcasted_iota(jnp.int32, (ns, 1), 0)
    tt = qi * tq + lax.broadcasted_iota(jnp.int32, (1, tq), 1)
    causal = m * NSA_SEL_LEN <= tt
    cur = lax.shift_right_logical(tt, 6)
    forced = (m == 0) | (m == cur) | (m == cur - 1)
    score = jnp.where(causal, jnp.where(forced, FORCE, imp), -FORCE)
    cnt = jnp.zeros((ns, tq), F32)
    for m2 in range(ns):
        r = score[m2:m2 + 1, :]
        ahead = (r > score) | ((r == score) & (m2 < m))
        cnt = cnt + jnp.where(ahead, 1.0, 0.0)
    sel = jnp.where((cnt < float(ksel)) & causal, 1.0, 0.0)
    selp = jnp.concatenate([sel, jnp.zeros((LANE - ns, tq), F32)], axis=0)
    sel_ref[0, 0] = selp.T.astype(sel_ref.dtype)


def nsa_cmp_select(z3, kv_cmp, ovt, *, tq=512):
    B, S, _ = z3.shape
    G = NSA_KV_GROUPS
    tq = min(tq, S)
    ns = S // NSA_SEL_LEN
    nc = kv_cmp.shape[3]
    kern = functools.partial(_nsa_cmp_kernel, tq=tq, ns=ns, ksel=min(NSA_SEL_TOPK, ns))
    return pl.pallas_call(
        kern,
        out_shape=[jax.ShapeDtypeStruct((B, S, NSA_HEADS * NSA_HEAD_DIM), BF16),
                   jax.ShapeDtypeStruct((B, G, S, LANE), BF16)],
        grid=(B, G, S // tq),
        in_specs=[pl.BlockSpec((1, tq, 4 * LANE), lambda b, g, i: (b, i, g)),
                  pl.BlockSpec((1, 1, 1, nc, LANE), lambda b, g, i: (0, b, g, 0, 0)),
                  pl.BlockSpec((1, 1, 1, nc, LANE), lambda b, g, i: (1, b, g, 0, 0)),
                  pl.BlockSpec((ns, nc), lambda b, g, i: (0, 0))],
        out_specs=[pl.BlockSpec((1, tq, 4 * LANE), lambda b, g, i: (b, i, g)),
                   pl.BlockSpec((1, 1, tq, LANE), lambda b, g, i: (b, g, i, 0))],
        compiler_params=_cp(("parallel", "parallel", "parallel")),
        name="nsa_cmp_select",
    )(z3, kv_cmp, kv_cmp, ovt)


def _stack_heads(q_ref, q4_ref, t):
    for h in range(NSA_HPG):
        q4_ref[h * t:(h + 1) * t, :] = q_ref[0, :, h * LANE:(h + 1) * LANE]


def _mask_heads(s, mask, t):
    return jnp.concatenate([jnp.where(mask, s[h * t:(h + 1) * t], NEG) for h in range(NSA_HPG)], axis=0)


def _nsa_sel_kernel(q_ref, k_ref, v_ref, sel_ref, e_ref, o_ref, q4_ref, m_ref, l_ref, acc_ref, *, t):
    qi = pl.program_id(2)
    _stack_heads(q_ref, q4_ref, t)
    sel = sel_ref[0, 0]
    row = lax.broadcasted_iota(jnp.int32, (t, t), 0)
    col = lax.broadcasted_iota(jnp.int32, (t, t), 1)

    def tile(j, diag, first):
        off = pl.multiple_of(j * t, t)
        mask = jnp.dot(sel, e_ref[j], preferred_element_type=F32) > 0.5
        if diag:
            mask = mask & (col <= row)
        s = lax.dot_general(q4_ref[...], k_ref[0, pl.ds(off, t), :], _NT, preferred_element_type=F32)
        _softmax_tile(_mask_heads(s, mask, t), v_ref[0, pl.ds(off, t), :], m_ref, l_ref, acc_ref, first)

    tile(qi, True, True)

    def body(j, carry):
        tile(j, False, False)
        return carry

    lax.fori_loop(0, qi, body, 0)
    o = _softmax_finish(l_ref, acc_ref)
    for h in range(NSA_HPG):
        o_ref[0, :, h * LANE:(h + 1) * LANE] = o[h * t:(h + 1) * t].astype(o_ref.dtype)


def _expand_blocks(S, t):
    key = np.arange(S).reshape(S // t, 1, t)
    blk = np.arange(LANE).reshape(1, LANE, 1)
    return jnp.asarray((key // NSA_SEL_LEN == blk).astype(np.float32), dtype=BF16)


def nsa_selected(z3, sel, *, t=256):
    B, S, _ = z3.shape
    G = NSA_KV_GROUPS
    t = min(t, S)
    n = S // t
    big = pl.BlockSpec((1, t, 4 * LANE), lambda b, g, i: (b, i, g))
    return pl.pallas_call(
        functools.partial(_nsa_sel_kernel, t=t),
        out_shape=jax.ShapeDtypeStruct((B, S, NSA_HEADS * NSA_HEAD_DIM), BF16),
        grid=(B, G, n),
        in_specs=[big,
                  pl.BlockSpec((1, S, LANE), lambda b, g, i: (b, 0, NSA_BLK_KS + g)),
                  pl.BlockSpec((1, S, LANE), lambda b, g, i: (b, 0, NSA_BLK_VS + g)),
                  pl.BlockSpec((1, 1, t, LANE), lambda b, g, i: (b, g, i, 0)),
                  pl.BlockSpec((n, LANE, t), lambda b, g, i: (0, 0, 0))],
        out_specs=big,
        scratch_shapes=[pltpu.VMEM((NSA_HPG * t, LANE), BF16), pltpu.VMEM((NSA_HPG * t, LANE), F32),
                        pltpu.VMEM((NSA_HPG * t, LANE), F32), pltpu.VMEM((NSA_HPG * t, LANE), F32)],
        compiler_params=_cp(("parallel", "parallel", "arbitrary")),
        name="nsa_selected",
    )(z3, z3, z3, sel, _expand_blocks(S, t))


def _nsa_win_kernel(q_ref, k_ref, v_ref, oc_ref, os_ref, gl_ref, bg_ref, o_ref, q4_ref, m_ref, l_ref, acc_ref,
                    *, t, nw):
    qi = pl.program_id(2)
    _stack_heads(q_ref, q4_ref, t)
    row = lax.broadcasted_iota(jnp.int32, (t, t), 0)
    col = lax.broadcasted_iota(jnp.int32, (t, t), 1)

    for w in range(nw):
        off = pl.multiple_of(jnp.maximum(qi - w, 0) * t, t)
        s = lax.dot_general(q4_ref[...], k_ref[0, pl.ds(off, t), :], _NT, preferred_element_type=F32)
        conds = []
        if w == 0:
            conds.append(col <= row)
        if (w + 1) * t > NSA_WINDOW:
            conds.append(col > row + (w * t - NSA_WINDOW))
        if w > 0:
            conds.append(qi >= w)
        mask = functools.reduce(jnp.logical_and, conds)
        s = _mask_heads(s, mask, t) if mask.ndim == 2 else jnp.where(mask, s, NEG)
        _softmax_tile(s, v_ref[0, pl.ds(off, t), :], m_ref, l_ref, acc_ref, w == 0)

    o_win = _softmax_finish(l_ref, acc_ref)
    gates = jax.nn.sigmoid(gl_ref[0].astype(F32) + bg_ref[0])
    for h in range(NSA_HPG):
        cols = slice(h * LANE, (h + 1) * LANE)
        o = (gates[:, h:h + 1] * oc_ref[0, :, cols].astype(F32)
             + gates[:, 4 + h:5 + h] * os_ref[0, :, cols].astype(F32)
             + gates[:, 8 + h:9 + h] * o_win[h * t:(h + 1) * t])
        o_ref[0, :, cols] = o.astype(o_ref.dtype)


def nsa_window_merge(z3, o_cmp, o_sel, bg, *, t=256):
    B, S, _ = z3.shape
    G = NSA_KV_GROUPS
    t = min(t, S)
    n = S // t
    assert NSA_WINDOW % t == 0
    nw = min(NSA_WINDOW // t + 1, n)
    big = pl.BlockSpec((1, t, 4 * LANE), lambda b, g, i: (b, i, g))
    return pl.pallas_call(
        functools.partial(_nsa_win_kernel, t=t, nw=nw),
        out_shape=jax.ShapeDtypeStruct((B, S, NSA_HEADS * NSA_HEAD_DIM), BF16),
        grid=(B, G, n),
        in_specs=[big,
                  pl.BlockSpec((1, S, LANE), lambda b, g, i: (b, 0, NSA_BLK_KW + g)),
                  pl.BlockSpec((1, S, LANE), lambda b, g, i: (b, 0, NSA_BLK_VW + g)),
                  big, big,
                  pl.BlockSpec((1, t, LANE), lambda b, g, i: (b, i, NSA_BLK_GATE + g)),
                  pl.BlockSpec((1, 1, LANE), lambda b, g, i: (g, 0, 0))],
        out_specs=big,
        scratch_shapes=[pltpu.VMEM((NSA_HPG * t, LANE), BF16), pltpu.VMEM((NSA_HPG * t, LANE), F32),
                        pltpu.VMEM((NSA_HPG * t, LANE), F32), pltpu.VMEM((NSA_HPG * t, LANE), F32)],
        compiler_params=_cp(("parallel", "parallel", "arbitrary")),
        name="nsa_window_merge",
    )(z3, z3, z3, o_cmp, o_sel, z3, bg)


def _rot_half_cols(w):
    half = w.shape[-1] // 2
    return jnp.concatenate([-w[..., half:], w[..., :half]], axis=-1)


def _prep_ab(w_in, w_alpha_up, b_alpha, w_uq, w_ukv):
    D = w_in.shape[0]
    q_g, k_g, v_g, g_g, a_lr, c_q, c_kv, k_r = _split_cols(w_in, AB_SPLITS)
    qk = jnp.concatenate([q_g.reshape(D, GLA_HEADS, GLA_DK), k_g.reshape(D, GLA_HEADS, GLA_DK)],
                         axis=-1).reshape(D, GLA_HEADS * LANE)
    tail = jnp.concatenate([k_r, _rot_half_cols(k_r), a_lr,
                            jnp.zeros((D, 512 - 2 * MLA_ROPE - GLA_GATE_RANK), w_in.dtype)], axis=-1)
    w = jnp.concatenate([qk, v_g, g_g, c_q, c_kv, tail], axis=-1).astype(BF16)
    wa = w_alpha_up.reshape(GLA_GATE_RANK, GLA_HEADS, GLA_DK).transpose(1, 0, 2)
    wa = jnp.concatenate([wa, wa], axis=-1)
    wa = jnp.pad(wa, ((0, 0), (0, LANE - GLA_GATE_RANK), (0, 0))).astype(BF16)
    ba = b_alpha.reshape(GLA_HEADS, 1, GLA_DK)
    ba = jnp.concatenate([ba, ba], axis=-1)
    wq = w_uq.reshape(MLA_Q_RANK, MLA_HEADS, MLA_QK)
    rope = wq[..., MLA_NOPE:]
    wq = jnp.concatenate([wq, _rot_half_cols(rope)], axis=-1).transpose(1, 0, 2).astype(BF16)
    wkv = w_ukv.reshape(MLA_KV_RANK, MLA_HEADS, MLA_NOPE + MLA_V).transpose(1, 0, 2).astype(BF16)
    return w, wa, ba, wq, wkv


def _prep_nsa(w_in, b_gate):
    D = w_in.shape[0]
    q, kc, vc, ks, vs, kw, vw, gl = _split_cols(w_in, NSA_SPLITS)
    G, HG = NSA_KV_GROUPS, NSA_HPG
    glp = gl.reshape(D, G, HG, 3).transpose(0, 1, 3, 2).reshape(D, G, 3 * HG)
    glp = jnp.pad(glp, ((0, 0), (0, 0), (0, LANE - 3 * HG))).reshape(D, G * LANE)
    w = jnp.concatenate([q, kc, ks, kw, vc, vs, vw, glp], axis=-1).astype(BF16)
    bg = b_gate.reshape(G, HG, 3).transpose(0, 2, 1).reshape(G, 1, 3 * HG)
    bg = jnp.pad(bg, ((0, 0), (0, 0), (0, LANE - 3 * HG)))
    return w, bg


def _overlap_t(S):
    nr = S // NSA_CMP_STRIDE
    ns = S // NSA_SEL_LEN
    c_start = np.arange(nr) * NSA_CMP_STRIDE
    c_end = c_start + NSA_CMP_LEN
    s_start = np.arange(ns) * NSA_SEL_LEN
    s_end = s_start + NSA_SEL_LEN
    ov = (c_start[None, :] < s_end[:, None]) & (c_end[None, :] > s_start[:, None])
    return jnp.asarray(ov.astype(np.float32), dtype=BF16)


def gla_mla_mixer(h, pre_w, B, S, c2, s2, w_in, w_alpha_up, b_alpha, gla_norm_w,
                  q_norm_w, w_uq, kv_norm_w, w_ukv):
    w, wa, ba, wq, wkv = _prep_ab(w_in, w_alpha_up, b_alpha, w_uq, w_ukv)
    z = norm_matmul(h, pre_w, w)
    z3 = z.reshape(B, S, AB_Z)
    o_gla = gla(z3, wa, ba, gla_norm_w)
    qm, km, vm = mla_proj(z, q_norm_w, kv_norm_w, wq, wkv, c2, s2, B, S)
    o_mla = mla_attention(qm, km, vm)
    return jnp.concatenate([o_gla, o_mla], axis=-1).reshape(B * S, -1)


def nsa_mixer(h, pre_w, B, S, c128, s128, w_in, b_gate, cmp_pos, cmp_w1, cmp_w2):
    w, bg = _prep_nsa(w_in, b_gate)
    z = norm_matmul_rope(h, pre_w, w, c128, s128)
    z3 = z.reshape(B, S, NSA_Z)
    G, dh = NSA_KV_GROUPS, NSA_HEAD_DIM

    def blocks(blk):
        t = z3[:, :, blk * LANE:(blk + G) * LANE].reshape(B, S, G, dh).transpose(0, 2, 1, 3)
        return t.reshape(B, G, S // NSA_CMP_STRIDE, NSA_CMP_STRIDE * dh)

    r = jnp.stack([blocks(NSA_BLK_KC), blocks(NSA_BLK_VC)])
    kv_cmp = nsa_compress(r, cmp_pos.reshape(2, 1, NSA_CMP_LEN * dh),
                          cmp_w1.astype(BF16), cmp_w2.astype(BF16))
    o_cmp, sel = nsa_cmp_select(z3, kv_cmp, _overlap_t(S))
    o_sel = nsa_selected(z3, sel)
    o = nsa_window_merge(z3, o_cmp, o_sel, bg)
    return o.reshape(B * S, -1)


def kernel(x, p, positions, ln_mix_pre, ln_mix_post, ln_ffn_pre, ln_ffn_post, ab_w_in, gla_w_alpha_up, gla_b_alpha, gla_norm_w, mla_q_norm_w, mla_w_uq, mla_kv_norm_w, mla_w_ukv, ab_w_out, nsa_w_in, nsa_b_gate, nsa_cmp_pos, nsa_cmp_w1, nsa_cmp_w2, nsa_w_out, ffn_w_gate, ffn_w_up, ffn_w_down, ple_w_gate, ple_b_gate, ple_w_proj):
    B, S, D = x.shape
    T = B * S
    depth = p.shape[0]
    c2, s2, c128, s128 = rope_tables(positions)
    h = x.reshape(T, D)
    for i in range(depth):
        j = i // 2
        if i % 2 == 0:
            mix = gla_mla_mixer(h, ln_mix_pre[i], B, S, c2, s2, ab_w_in[j], gla_w_alpha_up[j],
                                gla_b_alpha[j], gla_norm_w[j], mla_q_norm_w[j], mla_w_uq[j],
                                mla_kv_norm_w[j], mla_w_ukv[j])
            w_out = ab_w_out[j]
        else:
            mix = nsa_mixer(h, ln_mix_pre[i], B, S, c128, s128, nsa_w_in[j], nsa_b_gate[j],
                            nsa_cmp_pos[j], nsa_cmp_w1[j], nsa_cmp_w2[j])
            w_out = nsa_w_out[j]
        h = matmul_norm_residual(mix, w_out.astype(BF16), h, ln_mix_post[i], tk=mix.shape[1])
        act = norm_swiglu(h, ln_ffn_pre[i], ffn_w_gate[i].astype(BF16), ffn_w_up[i].astype(BF16))
        h = matmul_norm_residual(act, ffn_w_down[i].astype(BF16), h, ln_ffn_post[i], tk=FFN_HIDDEN // 4)
        h = ple(h, ple_w_gate[i].astype(BF16), ple_b_gate[i], p[i].reshape(T, -1),
                ple_w_proj[i].astype(BF16))
    return h.reshape(B, S, D)
```

```python
import functools

import numpy as np
import jax
import jax.numpy as jnp
from jax import lax
from jax.experimental import pallas as pl
from jax.experimental.pallas import tpu as pltpu

F32 = jnp.float32
BF16 = jnp.bfloat16

D_MODEL = 2048
PLE_DIM = 256
ROPE_THETA = 10000.0
NORM_EPS = 1e-6
NEG = -1e30
FORCE = 1e6

GLA_HEADS = 8
GLA_DK = 64
GLA_DV = 128
GLA_GATE_RANK = 16
GLA_TAU = 16.0
GLA_CHUNK = 64

MLA_HEADS = 8
MLA_Q_RANK = 512
MLA_KV_RANK = 512
MLA_NOPE = 128
MLA_ROPE = 64
MLA_V = 128
MLA_QK = MLA_NOPE + MLA_ROPE

NSA_HEADS = 16
NSA_KV_GROUPS = 4
NSA_HPG = NSA_HEADS // NSA_KV_GROUPS
NSA_HEAD_DIM = 128
NSA_CMP_LEN = 32
NSA_CMP_STRIDE = 16
NSA_SEL_LEN = 64
NSA_SEL_TOPK = 16
NSA_WINDOW = 512

FFN_HIDDEN = 5632

AB_SPLITS = (GLA_HEADS * GLA_DK, GLA_HEADS * GLA_DK, GLA_HEADS * GLA_DV, GLA_HEADS * GLA_DV,
             GLA_GATE_RANK, MLA_Q_RANK, MLA_KV_RANK, MLA_ROPE)
NSA_KV_W = NSA_KV_GROUPS * NSA_HEAD_DIM
NSA_SPLITS = (NSA_HEADS * NSA_HEAD_DIM,) + (NSA_KV_W,) * 6 + (3 * NSA_HEADS,)

LOG2E = 1.4426950408889634
LANE = 128
VMEM_LIMIT = 56 * 1024 * 1024

AB_Z = 4608
AB_BLK_QK = 0
AB_BLK_V = 8
AB_BLK_G = 16
AB_BLK_CQ = 24
AB_BLK_CKV = 28
AB_BLK_KR = 32
AB_BLK_ALR = 33

NSA_Z = 5632
NSA_BLK_Q = 0
NSA_BLK_KC = 16
NSA_BLK_KS = 20
NSA_BLK_KW = 24
NSA_BLK_VC = 28
NSA_BLK_VS = 32
NSA_BLK_VW = 36
NSA_BLK_GATE = 40
NSA_ROPE_TILES = 7
NSA_Q_TILES = 4


def _cp(sem):
    return pltpu.CompilerParams(dimension_semantics=sem, vmem_limit_bytes=VMEM_LIMIT)


def _rms(x, w):
    return x * lax.rsqrt(jnp.mean(x * x, axis=-1, keepdims=True) + NORM_EPS) * w


def _split_cols(z, widths):
    out, off = [], 0
    for w in widths:
        out.append(z[..., off:off + w])
        off += w
    return out


def _tables_kernel(pos_ref, inv_ref, c2_ref, s2_ref, c128_ref, s128_ref):
    pos = pos_ref[...].astype(F32)
    lane = lax.broadcasted_iota(jnp.int32, (1, LANE), 1)
    lo = lane < 64
    a64 = pos * inv_ref[0:1, :]
    c2_ref[...] = jnp.where(lo, jnp.cos(a64), 0.0)
    s2_ref[...] = jnp.where(lo, jnp.sin(a64), 0.0)
    a128 = pos * inv_ref[1:2, :]
    s = jnp.sin(a128)
    c128_ref[...] = jnp.cos(a128)
    s128_ref[...] = jnp.where(lo, -s, s)


def rope_tables(positions):
    T = positions.size
    inv32 = jnp.power(ROPE_THETA, -jnp.arange(0, MLA_ROPE, 2, dtype=F32) / MLA_ROPE)
    inv64 = jnp.power(ROPE_THETA, -jnp.arange(0, NSA_HEAD_DIM, 2, dtype=F32) / NSA_HEAD_DIM)
    inv = jnp.zeros((8, LANE), F32)
    inv = inv.at[0, :64].set(jnp.concatenate([inv32, inv32]))
    inv = inv.at[1, :].set(jnp.concatenate([inv64, inv64]))
    tm = min(T, 1024)
    spec = pl.BlockSpec((tm, LANE), lambda i: (i, 0))
    return pl.pallas_call(
        _tables_kernel,
        out_shape=[jax.ShapeDtypeStruct((T, LANE), F32)] * 4,
        grid=(T // tm,),
        in_specs=[pl.BlockSpec((tm, 1), lambda i: (i, 0)), pl.BlockSpec((8, LANE), lambda i: (0, 0))],
        out_specs=[spec] * 4,
        compiler_params=_cp(("parallel",)),
        name="rope_tables",
    )(positions.reshape(T, 1), inv)


def _norm_mm_kernel(x_ref, nw_ref, w_ref, o_ref, xn_ref):
    @pl.when(pl.program_id(1) == 0)
    def _():
        xn_ref[...] = _rms(x_ref[...], nw_ref[...]).astype(BF16)

    o_ref[...] = jnp.dot(xn_ref[...], w_ref[...], preferred_element_type=F32).astype(o_ref.dtype)


def _norm_mm_rope_kernel(x_ref, nw_ref, w_ref, cos_ref, sin_ref, o_ref, xn_ref, *, tn, scale):
    j = pl.program_id(1)

    @pl.when(j == 0)
    def _():
        xn_ref[...] = _rms(x_ref[...], nw_ref[...]).astype(BF16)

    y = jnp.dot(xn_ref[...], w_ref[...], preferred_element_type=F32)
    mult = jnp.where(j < NSA_Q_TILES, scale, 1.0)
    is_rope = j < NSA_ROPE_TILES
    cos = jnp.where(is_rope, cos_ref[...] * mult, 1.0)
    sin = jnp.where(is_rope, sin_ref[...] * mult, 0.0)
    for c in range(tn // LANE):
        seg = y[:, c * LANE:(c + 1) * LANE]
        o_ref[:, c * LANE:(c + 1) * LANE] = (seg * cos + pltpu.roll(seg, 64, 1) * sin).astype(o_ref.dtype)


def _norm_swiglu_kernel(x_ref, nw_ref, wg_ref, wu_ref, o_ref, xn_ref):
    @pl.when(pl.program_id(1) == 0)
    def _():
        xn_ref[...] = _rms(x_ref[...], nw_ref[...]).astype(BF16)

    xn = xn_ref[...]
    g = jnp.dot(xn, wg_ref[...], preferred_element_type=F32)
    u = jnp.dot(xn, wu_ref[...], preferred_element_type=F32)
    o_ref[...] = (g * jax.nn.sigmoid(g) * u).astype(o_ref.dtype)


def _row_tile(T, want):
    return min(T, want)


def norm_matmul(x, nw, w, *, tm=1024, tn=512):
    T, D = x.shape
    N = w.shape[1]
    tm = _row_tile(T, tm)
    return pl.pallas_call(
        _norm_mm_kernel,
        out_shape=jax.ShapeDtypeStruct((T, N), BF16),
        grid=(T // tm, N // tn),
        in_specs=[pl.BlockSpec((tm, D), lambda i, j: (i, 0)),
                  pl.BlockSpec((1, D), lambda i, j: (0, 0)),
                  pl.BlockSpec((D, tn), lambda i, j: (0, j))],
        out_specs=pl.BlockSpec((tm, tn), lambda i, j: (i, j)),
        scratch_shapes=[pltpu.VMEM((tm, D), BF16)],
        compiler_params=_cp(("parallel", "arbitrary")),
        name="norm_matmul",
    )(x, nw.reshape(1, D), w)


def norm_matmul_rope(x, nw, w, cos, sin, *, tm=1024, tn=512):
    T, D = x.shape
    N = w.shape[1]
    tm = _row_tile(T, tm)
    kern = functools.partial(_norm_mm_rope_kernel, tn=tn, scale=NSA_HEAD_DIM ** -0.5 * LOG2E)
    return pl.pallas_call(
        kern,
        out_shape=jax.ShapeDtypeStruct((T, N), BF16),
        grid=(T // tm, N // tn),
        in_specs=[pl.BlockSpec((tm, D), lambda i, j: (i, 0)),
                  pl.BlockSpec((1, D), lambda i, j: (0, 0)),
                  pl.BlockSpec((D, tn), lambda i, j: (0, j)),
                  pl.BlockSpec((tm, LANE), lambda i, j: (i, 0)),
                  pl.BlockSpec((tm, LANE), lambda i, j: (i, 0))],
        out_specs=pl.BlockSpec((tm, tn), lambda i, j: (i, j)),
        scratch_shapes=[pltpu.VMEM((tm, D), BF16)],
        compiler_params=_cp(("parallel", "arbitrary")),
        name="norm_matmul_rope",
    )(x, nw.reshape(1, D), w, cos, sin)


def norm_swiglu(x, nw, wg, wu, layer, *, tm=1024, tn=512):
    T, D = x.shape
    N = wg.shape[2]
    tm = _row_tile(T, tm)
    wspec = pl.BlockSpec((None, D, tn), lambda i, j: (layer, 0, j))
    return pl.pallas_call(
        _norm_swiglu_kernel,
        out_shape=jax.ShapeDtypeStruct((T, N), BF16),
        grid=(T // tm, N // tn),
        in_specs=[pl.BlockSpec((tm, D), lambda i, j: (i, 0)),
                  pl.BlockSpec((1, D), lambda i, j: (0, 0)),
                  wspec, wspec],
        out_specs=pl.BlockSpec((tm, tn), lambda i, j: (i, j)),
        scratch_shapes=[pltpu.VMEM((tm, D), BF16)],
        compiler_params=_cp(("parallel", "arbitrary")),
        name="norm_swiglu",
    )(x, nw.reshape(1, D), wg, wu)


def _mm_norm_res_kernel(a_ref, w_ref, h_ref, nw_ref, o_ref, acc_ref):
    k = pl.program_id(1)

    @pl.when(k == 0)
    def _():
        acc_ref[...] = jnp.zeros_like(acc_ref)

    acc_ref[...] += jnp.dot(a_ref[...], w_ref[...], preferred_element_type=F32)

    @pl.when(k == pl.num_programs(1) - 1)
    def _():
        o_ref[...] = h_ref[...] + _rms(acc_ref[...], nw_ref[...])


def _mm_norm_res_parts_kernel(*refs, widths):
    a_refs = refs[:len(widths)]
    w_ref, h_ref, nw_ref, o_ref = refs[len(widths):]
    m, off = None, 0
    for a_ref, wd in zip(a_refs, widths):
        part = jnp.dot(a_ref[...], w_ref[off:off + wd, :], preferred_element_type=F32)
        m = part if m is None else m + part
        off += wd
    o_ref[...] = h_ref[...] + _rms(m, nw_ref[...])


def matmul_parts_norm_residual(parts, w, layer, h, nw, *, tm=512):
    T = parts[0].shape[0]
    widths = tuple(a.shape[1] for a in parts)
    K, D = w.shape[1:]
    assert sum(widths) == K
    tm = _row_tile(T, tm)
    return pl.pallas_call(
        functools.partial(_mm_norm_res_parts_kernel, widths=widths),
        out_shape=jax.ShapeDtypeStruct((T, D), F32),
        grid=(T // tm,),
        in_specs=[pl.BlockSpec((tm, wd), lambda i: (i, 0)) for wd in widths]
        + [pl.BlockSpec((None, K, D), lambda i: (layer, 0, 0)),
           pl.BlockSpec((tm, D), lambda i: (i, 0)),
           pl.BlockSpec((1, D), lambda i: (0, 0))],
        out_specs=pl.BlockSpec((tm, D), lambda i: (i, 0)),
        compiler_params=_cp(("parallel",)),
        name="matmul_parts_norm_residual",
    )(*parts, w, h, nw.reshape(1, D))


def matmul_norm_residual(a, w, layer, h, nw, *, tm=512, tk=512):
    T, K = a.shape
    D = w.shape[2]
    tm = _row_tile(T, tm)
    return pl.pallas_call(
        _mm_norm_res_kernel,
        out_shape=jax.ShapeDtypeStruct((T, D), F32),
        grid=(T // tm, K // tk),
        in_specs=[pl.BlockSpec((tm, tk), lambda i, k: (i, k)),
                  pl.BlockSpec((None, tk, D), lambda i, k: (layer, k, 0)),
                  pl.BlockSpec((tm, D), lambda i, k: (i, 0)),
                  pl.BlockSpec((1, D), lambda i, k: (0, 0))],
        out_specs=pl.BlockSpec((tm, D), lambda i, k: (i, 0)),
        scratch_shapes=[pltpu.VMEM((tm, D), F32)],
        compiler_params=_cp(("parallel", "arbitrary")),
        name="matmul_norm_residual",
    )(a, w, h, nw.reshape(1, D))


def _ple_kernel(h_ref, wg_ref, bg_ref, p_ref, wp_ref, o_ref, hb_ref, *, tn):
    j = pl.program_id(1)

    @pl.when(j == 0)
    def _():
        hb_ref[...] = h_ref[...].astype(BF16)

    g = jnp.dot(hb_ref[...], wg_ref[...], preferred_element_type=F32) + bg_ref[...]
    pp = jnp.dot(p_ref[...].astype(BF16), wp_ref[...], preferred_element_type=F32)
    hs = h_ref[:, pl.ds(pl.multiple_of(j * tn, tn), tn)]
    o_ref[...] = hs + jax.nn.sigmoid(g) * pp


def ple(h, wg, bg, p, wp, layer, *, tm=1024, tn=512):
    T, D = h.shape
    P = p.shape[2]
    tm = _row_tile(T, tm)
    return pl.pallas_call(
        functools.partial(_ple_kernel, tn=tn),
        out_shape=jax.ShapeDtypeStruct((T, D), F32),
        grid=(T // tm, D // tn),
        in_specs=[pl.BlockSpec((tm, D), lambda i, j: (i, 0)),
                  pl.BlockSpec((None, D, tn), lambda i, j: (layer, 0, j)),
                  pl.BlockSpec((1, tn), lambda i, j: (0, j)),
                  pl.BlockSpec((None, tm, P), lambda i, j: (layer, i, 0)),
                  pl.BlockSpec((None, P, tn), lambda i, j: (layer, 0, j))],
        out_specs=pl.BlockSpec((tm, tn), lambda i, j: (i, j)),
        scratch_shapes=[pltpu.VMEM((tm, D), BF16)],
        compiler_params=_cp(("parallel", "arbitrary")),
        name="ple",
    )(h, wg, bg.reshape(1, D), p, wp)


def _split3(x):
    h1 = x.astype(BF16)
    r1 = x - h1.astype(F32)
    h2 = r1.astype(BF16)
    h3 = (r1 - h2.astype(F32)).astype(BF16)
    return h1, h2, h3


def _gla_kernel(qk_ref, v_ref, g_ref, alr_ref, wa_ref, ba_ref, nw_ref, o_ref, st_ref, *, tr):
    L = GLA_CHUNK

    @pl.when(pl.program_id(2) == 0)
    def _():
        st_ref[...] = jnp.zeros_like(st_ref)

    x = jnp.dot(alr_ref[0], wa_ref[0], preferred_element_type=F32) + ba_ref[0]
    log_a = (jnp.minimum(x, 0.0) - jnp.log1p(jnp.exp(-jnp.abs(x)))) * (1.0 / GLA_TAU)

    row = lax.broadcasted_iota(jnp.int32, (tr, tr), 0)
    col = lax.broadcasted_iota(jnp.int32, (tr, tr), 1)
    same = lax.shift_right_logical(row, 6) == lax.shift_right_logical(col, 6)
    causal = same & (row >= col)
    tri = jnp.where(causal, 1.0, 0.0).astype(BF16)
    ones = jnp.where(same, 1.0, 0.0).astype(BF16)
    lo = lax.broadcasted_iota(jnp.int32, (1, LANE), 1) < GLA_DK
    tn = (((0,), (0,)), ((), ()))

    parts = _split3(log_a)
    b = functools.reduce(jnp.add, [jnp.dot(tri, a, preferred_element_type=F32) for a in parts])
    b_end = functools.reduce(jnp.add, [jnp.dot(ones, a, preferred_element_type=F32) for a in parts])
    blk = qk_ref[0].astype(F32)
    swp = pltpu.roll(blk, 64, 1)
    q_dec = jnp.where(lo, blk * jnp.exp(b) * (GLA_DK ** -0.5), 0.0).astype(BF16)
    k_inv = jnp.where(lo, swp * jnp.exp(-b), 0.0).astype(BF16)
    k_end = jnp.where(lo, swp * jnp.exp(b_end - b), 0.0).astype(BF16)
    decay = jnp.exp(b_end)
    v = v_ref[0]
    attn = lax.dot_general(q_dec, k_inv, _NT, preferred_element_type=F32)
    attn = jnp.where(causal, attn, 0.0).astype(BF16)
    o_intra = jnp.dot(attn, v, preferred_element_type=F32)

    st = st_ref[...]
    outs = []
    for c in range(tr // L):
        rows = slice(c * L, (c + 1) * L)
        outs.append(o_intra[rows] + lax.dot_general(q_dec[rows], st.astype(BF16), _NT,
                                                    preferred_element_type=F32))
        st = st * decay[c * L:c * L + 1] + lax.dot_general(v[rows], k_end[rows], tn,
                                                           preferred_element_type=F32)
    st_ref[...] = st
    o = jnp.concatenate(outs, axis=0)
    g = g_ref[0].astype(F32)
    o_ref[0] = (_rms(o, nw_ref[...]) * (g * jax.nn.sigmoid(g))).astype(o_ref.dtype)


def gla(z3, wa, ba, nw, *, tr=256):
    B, S, _ = z3.shape
    tr = min(tr, S)
    H = GLA_HEADS
    return pl.pallas_call(
        functools.partial(_gla_kernel, tr=tr),
        out_shape=jax.ShapeDtypeStruct((B, S, H * GLA_DV), BF16),
        grid=(B, H, S // tr),
        in_specs=[pl.BlockSpec((1, tr, LANE), lambda b, h, r: (b, r, AB_BLK_QK + h)),
                  pl.BlockSpec((1, tr, LANE), lambda b, h, r: (b, r, AB_BLK_V + h)),
                  pl.BlockSpec((1, tr, LANE), lambda b, h, r: (b, r, AB_BLK_G + h)),
                  pl.BlockSpec((1, tr, LANE), lambda b, h, r: (b, r, AB_BLK_ALR)),
                  pl.BlockSpec((1, LANE, LANE), lambda b, h, r: (h, 0, 0)),
                  pl.BlockSpec((1, 1, LANE), lambda b, h, r: (h, 0, 0)),
                  pl.BlockSpec((1, LANE), lambda b, h, r: (0, 0))],
        out_specs=pl.BlockSpec((1, tr, LANE), lambda b, h, r: (b, r, h)),
        scratch_shapes=[pltpu.VMEM((GLA_DV, LANE), F32)],
        compiler_params=_cp(("parallel", "parallel", "arbitrary")),
        name="gla",
    )(z3, z3, z3, z3, wa, ba, nw.reshape(1, GLA_DV))


def _mla_proj_kernel(cq_ref, ckv_ref, kr_ref, qnw_ref, kvnw_ref, wq_ref, wkv_ref, c2_ref, s2_ref,
                     q_ref, k_ref, v_ref, cqn_ref, ckvn_ref, kro_ref):
    @pl.when(pl.program_id(1) == 0)
    def _():
        cqn_ref[...] = _rms(cq_ref[...].astype(F32), qnw_ref[...]).astype(BF16)
        ckvn_ref[...] = _rms(ckv_ref[...].astype(F32), kvnw_ref[...]).astype(BF16)
        kr = kr_ref[...].astype(F32)
        kro_ref[...] = (kr * c2_ref[...] + pltpu.roll(kr, 64, 1) * s2_ref[...]).astype(BF16)

    scale = MLA_QK ** -0.5 * LOG2E
    yq = jnp.dot(cqn_ref[...], wq_ref[0], preferred_element_type=F32)
    y2 = yq[:, LANE:]
    qr = y2 * c2_ref[...] + pltpu.roll(y2, 64, 1) * s2_ref[...]
    q_ref[0, 0, :, :MLA_NOPE] = (yq[:, :LANE] * scale).astype(BF16)
    q_ref[0, 0, :, MLA_NOPE:] = (qr[:, :MLA_ROPE] * scale).astype(BF16)
    ykv = jnp.dot(ckvn_ref[...], wkv_ref[0], preferred_element_type=F32)
    k_ref[0, 0, :, :MLA_NOPE] = ykv[:, :LANE].astype(BF16)
    k_ref[0, 0, :, MLA_NOPE:] = kro_ref[:, :MLA_ROPE]
    v_ref[0, 0] = ykv[:, LANE:].astype(BF16)


def mla_proj(z, qnw, kvnw, wq, wkv, c2, s2, B, S, *, tm=512):
    T = z.shape[0]
    tm = min(tm, S)
    H = MLA_HEADS
    nb = S // tm

    def omap(i, h):
        return (i // nb, h, i % nb, 0)

    return pl.pallas_call(
        _mla_proj_kernel,
        out_shape=[jax.ShapeDtypeStruct((B, H, S, MLA_QK), BF16),
                   jax.ShapeDtypeStruct((B, H, S, MLA_QK), BF16),
                   jax.ShapeDtypeStruct((B, H, S, MLA_V), BF16)],
        grid=(T // tm, H),
        in_specs=[pl.BlockSpec((tm, MLA_Q_RANK), lambda i, h: (i, AB_BLK_CQ // 4)),
                  pl.BlockSpec((tm, MLA_KV_RANK), lambda i, h: (i, AB_BLK_CKV // 4)),
                  pl.BlockSpec((tm, LANE), lambda i, h: (i, AB_BLK_KR)),
                  pl.BlockSpec((1, MLA_Q_RANK), lambda i, h: (0, 0)),
                  pl.BlockSpec((1, MLA_KV_RANK), lambda i, h: (0, 0)),
                  pl.BlockSpec((1, MLA_Q_RANK, 2 * LANE), lambda i, h: (h, 0, 0)),
                  pl.BlockSpec((1, MLA_KV_RANK, 2 * LANE), lambda i, h: (h, 0, 0)),
                  pl.BlockSpec((tm, LANE), lambda i, h: (i, 0)),
                  pl.BlockSpec((tm, LANE), lambda i, h: (i, 0))],
        out_specs=[pl.BlockSpec((1, 1, tm, MLA_QK), omap),
                   pl.BlockSpec((1, 1, tm, MLA_QK), omap),
                   pl.BlockSpec((1, 1, tm, MLA_V), omap)],
        scratch_shapes=[pltpu.VMEM((tm, MLA_Q_RANK), BF16), pltpu.VMEM((tm, MLA_KV_RANK), BF16),
                        pltpu.VMEM((tm, LANE), BF16)],
        compiler_params=_cp(("parallel", "arbitrary")),
        name="mla_proj",
    )(z, z, z, qnw.reshape(1, -1), kvnw.reshape(1, -1), wq, wkv, c2, s2)


_NT = (((1,), (1,)), ((), ()))


def _softmax_tile(s, v, m_ref, l_ref, acc_ref, first):
    M, tk = s.shape
    chunks = [s[:, c * LANE:(c + 1) * LANE] for c in range(tk // LANE)]
    mrow = jnp.max(functools.reduce(jnp.maximum, chunks), axis=-1, keepdims=True)
    if first:
        m_new = jnp.broadcast_to(mrow, (M, LANE))
    else:
        m_prev = m_ref[...]
        m_new = jnp.maximum(m_prev, mrow)
    ps = [jnp.exp2(c - m_new) for c in chunks]
    lsum = functools.reduce(jnp.add, ps)
    p = (jnp.concatenate(ps, axis=1) if len(ps) > 1 else ps[0]).astype(BF16)
    pv = jnp.dot(p, v, preferred_element_type=F32)
    if first:
        l_ref[...] = lsum
        acc_ref[...] = pv
    else:
        alpha = jnp.exp2(m_prev - m_new)
        l_ref[...] = alpha * l_ref[...] + lsum
        acc_ref[...] = alpha * acc_ref[...] + pv
    m_ref[...] = m_new


def _softmax_finish(l_ref, acc_ref):
    return acc_ref[...] / jnp.sum(l_ref[...], axis=-1, keepdims=True)


def _mla_attn_kernel(q_ref, k_ref, v_ref, o_ref, m_ref, l_ref, acc_ref, *, tq, tk):
    qi = pl.program_id(2)
    r = tq // tk
    q = q_ref[0, 0]
    row = lax.broadcasted_iota(jnp.int32, (tq, tk), 0)
    col = lax.broadcasted_iota(jnp.int32, (tq, tk), 1)

    def tile(j, d, first):
        off = pl.multiple_of(j * tk, tk)
        s = lax.dot_general(q, k_ref[0, 0, pl.ds(off, tk), :], _NT, preferred_element_type=F32)
        if d is not None:
            s = jnp.where(col + d * tk <= row, s, NEG)
        _softmax_tile(s, v_ref[0, 0, pl.ds(off, tk), :], m_ref, l_ref, acc_ref, first)

    for d in range(r):
        tile(qi * r + d, d, d == 0)

    def body(j, carry):
        tile(j, None, False)
        return carry

    lax.fori_loop(0, qi * r, body, 0)
    o_ref[0] = _softmax_finish(l_ref, acc_ref).astype(o_ref.dtype)


def mla_attention(q, k, v, *, tq=512, tk=256):
    B, H, S, _ = q.shape
    tq = min(tq, S)
    tk = min(tk, tq)
    return pl.pallas_call(
        functools.partial(_mla_attn_kernel, tq=tq, tk=tk),
        out_shape=jax.ShapeDtypeStruct((B, S, H * MLA_V), BF16),
        grid=(B, H, S // tq),
        in_specs=[pl.BlockSpec((1, 1, tq, MLA_QK), lambda b, h, i: (b, h, i, 0)),
                  pl.BlockSpec((1, 1, S, MLA_QK), lambda b, h, i: (b, h, 0, 0)),
                  pl.BlockSpec((1, 1, S, MLA_V), lambda b, h, i: (b, h, 0, 0))],
        out_specs=pl.BlockSpec((1, tq, MLA_V), lambda b, h, i: (b, i, h)),
        scratch_shapes=[pltpu.VMEM((tq, LANE), F32), pltpu.VMEM((tq, LANE), F32),
                        pltpu.VMEM((tq, MLA_V), F32)],
        compiler_params=_cp(("parallel", "parallel", "arbitrary")),
        name="mla_attention",
    )(q, k, v)


def _nsa_compress_kernel(r_ref, pos_ref, w1_ref, w2_ref, o_ref):
    r = r_ref[0, 0, 0]
    half = r.shape[1]
    nr = r.shape[0]
    a = jnp.dot(r, w1_ref[0, :half, :], preferred_element_type=F32)
    b = jnp.dot(r, w1_ref[0, half:, :], preferred_element_type=F32)
    pos = jnp.broadcast_to(pos_ref[0], (8, 2 * half)).astype(BF16)
    c = jnp.dot(pos, w1_ref[0], preferred_element_type=F32)[0:1, :]
    pre = a + pltpu.roll(b, nr - 1, 0) + c
    o_ref[0, 0, 0] = jnp.dot(jax.nn.gelu(pre).astype(BF16), w2_ref[0],
                             preferred_element_type=F32).astype(o_ref.dtype)


def nsa_compress(r, pos, w1, w2):
    _, B, G, NR, W = r.shape
    dh = NSA_HEAD_DIM
    return pl.pallas_call(
        _nsa_compress_kernel,
        out_shape=jax.ShapeDtypeStruct((2, B, G, NR, dh), BF16),
        grid=(2, B, G),
        in_specs=[pl.BlockSpec((1, 1, 1, NR, W), lambda c, b, g: (c, b, g, 0, 0)),
                  pl.BlockSpec((1, 1, 2 * W), lambda c, b, g: (c, 0, 0)),
                  pl.BlockSpec((1, 2 * W, dh), lambda c, b, g: (c, 0, 0)),
                  pl.BlockSpec((1, dh, dh), lambda c, b, g: (c, 0, 0))],
        out_specs=pl.BlockSpec((1, 1, 1, NR, dh), lambda c, b, g: (c, b, g, 0, 0)),
        compiler_params=_cp(("parallel", "parallel", "parallel")),
        name="nsa_compress",
    )(r, pos, w1, w2)


def _nsa_cmp_kernel(q_ref, kc_ref, vc_ref, ovt_ref, o_ref, sel_ref, *, tq, ns, ksel):
    qi = pl.program_id(2)
    nc = kc_ref.shape[3]
    kc = kc_ref[0, 0, 0]
    vc = vc_ref[0, 0, 0]
    t = qi * tq + lax.broadcasted_iota(jnp.int32, (tq, 1), 0)
    n = lax.broadcasted_iota(jnp.int32, (1, nc), 1)
    ok = (n * NSA_CMP_STRIDE + (NSA_CMP_LEN - 1)) <= t
    nt = (((1,), (1,)), ((), ()))
    psum = jnp.zeros((tq, nc), F32)
    for j in range(NSA_HPG):
        q = q_ref[0, :, j * LANE:(j + 1) * LANE]
        s = lax.dot_general(q, kc, nt, preferred_element_type=F32)
        s = jnp.where(ok, s, NEG)
        e = jnp.where(ok, jnp.exp2(s - jnp.max(s, axis=-1, keepdims=True)), 0.0)
        d = jnp.sum(e, axis=-1, keepdims=True)
        p = e * jnp.where(d > 0.0, 1.0 / d, 0.0)
        o_ref[0, :, j * LANE:(j + 1) * LANE] = jnp.dot(p.astype(BF16), vc,
                                                        preferred_element_type=F32).astype(o_ref.dtype)
        psum = psum + p
    ph = psum.astype(BF16)
    plo = (psum - ph.astype(F32)).astype(BF16)
    ovt = ovt_ref[...]
    imp = (lax.dot_general(ovt, ph, nt, preferred_element_type=F32)
           + lax.dot_general(ovt, plo, nt, preferred_element_type=F32))
    m = lax.broadcasted_iota(jnp.int32, (ns, 1), 0)
    tt = qi * tq + lax.broadcasted_iota(jnp.int32, (1, tq), 1)
    causal = m * NSA_SEL_LEN <= tt
    cur = lax.shift_right_logical(tt, 6)
    forced = (m == 0) | (m == cur) | (m == cur - 1)
    score = jnp.where(causal, jnp.where(forced, FORCE, imp), -FORCE)
    cnt = jnp.zeros((ns, tq), F32)
    for m2 in range(ns):
        r = score[m2:m2 + 1, :]
        ahead = (r > score) | ((r == score) & (m2 < m))
        cnt = cnt + jnp.where(ahead, 1.0, 0.0)
    sel = jnp.where((cnt < float(ksel)) & causal, 1.0, 0.0)
    selp = jnp.concatenate([sel, jnp.zeros((LANE - ns, tq), F32)], axis=0)
    sel_ref[0, 0] = selp.T.astype(sel_ref.dtype)


def nsa_cmp_select(z3, kv_cmp, ovt, *, tq=512):
    B, S, _ = z3.shape
    G = NSA_KV_GROUPS
    tq = min(tq, S)
    ns = S // NSA_SEL_LEN
    nc = kv_cmp.shape[3]
    kern = functools.partial(_nsa_cmp_kernel, tq=tq, ns=ns, ksel=min(NSA_SEL_TOPK, ns))
    return pl.pallas_call(
        kern,
        out_shape=[jax.ShapeDtypeStruct((B, S, NSA_HEADS * NSA_HEAD_DIM), BF16),
                   jax.ShapeDtypeStruct((B, G, S, LANE), BF16)],
        grid=(B, G, S // tq),
        in_specs=[pl.BlockSpec((1, tq, 4 * LANE), lambda b, g, i: (b, i, g)),
                  pl.BlockSpec((1, 1, 1, nc, LANE), lambda b, g, i: (0, b, g, 0, 0)),
                  pl.BlockSpec((1, 1, 1, nc, LANE), lambda b, g, i: (1, b, g, 0, 0)),
                  pl.BlockSpec((ns, nc), lambda b, g, i: (0, 0))],
        out_specs=[pl.BlockSpec((1, tq, 4 * LANE), lambda b, g, i: (b, i, g)),
                   pl.BlockSpec((1, 1, tq, LANE), lambda b, g, i: (b, g, i, 0))],
        compiler_params=_cp(("parallel", "parallel", "parallel")),
        name="nsa_cmp_select",
    )(z3, kv_cmp, kv_cmp, ovt)


def _stack_heads(q_ref, q4_ref, t):
    for h in range(NSA_HPG):
        q4_ref[h * t:(h + 1) * t, :] = q_ref[0, :, h * LANE:(h + 1) * LANE]


def _mask_heads(s, mask, t):
    return jnp.concatenate([jnp.where(mask, s[h * t:(h + 1) * t], NEG) for h in range(NSA_HPG)], axis=0)


def _nsa_sel_kernel(q_ref, k_ref, v_ref, sel_ref, e_ref, o_ref, q4_ref, m_ref, l_ref, acc_ref, *, t):
    qi = pl.program_id(2)
    _stack_heads(q_ref, q4_ref, t)
    sel = sel_ref[0, 0]
    row = lax.broadcasted_iota(jnp.int32, (t, t), 0)
    col = lax.broadcasted_iota(jnp.int32, (t, t), 1)

    def tile(j, diag, first):
        off = pl.multiple_of(j * t, t)
        mask = jnp.dot(sel, e_ref[j], preferred_element_type=F32) > 0.5
        if diag:
            mask = mask & (col <= row)
        s = lax.dot_general(q4_ref[...], k_ref[0, pl.ds(off, t), :], _NT, preferred_element_type=F32)
        _softmax_tile(_mask_heads(s, mask, t), v_ref[0, pl.ds(off, t), :], m_ref, l_ref, acc_ref, first)

    tile(qi, True, True)

    def body(j, carry):
        tile(j, False, False)
        return carry

    lax.fori_loop(0, qi, body, 0)
    o = _softmax_finish(l_ref, acc_ref)
    for h in range(NSA_HPG):
        o_ref[0, :, h * LANE:(h + 1) * LANE] = o[h * t:(h + 1) * t].astype(o_ref.dtype)


def _expand_blocks(S, t):
    key = np.arange(S).reshape(S // t, 1, t)
    blk = np.arange(LANE).reshape(1, LANE, 1)
    return jnp.asarray((key // NSA_SEL_LEN == blk).astype(np.float32), dtype=BF16)


def nsa_selected(z3, sel, *, t=256):
    B, S, _ = z3.shape
    G = NSA_KV_GROUPS
    t = min(t, S)
    n = S // t
    big = pl.BlockSpec((1, t, 4 * LANE), lambda b, g, i: (b, i, g))
    return pl.pallas_call(
        functools.partial(_nsa_sel_kernel, t=t),
        out_shape=jax.ShapeDtypeStruct((B, S, NSA_HEADS * NSA_HEAD_DIM), BF16),
        grid=(B, G, n),
        in_specs=[big,
                  pl.BlockSpec((1, S, LANE), lambda b, g, i: (b, 0, NSA_BLK_KS + g)),
                  pl.BlockSpec((1, S, LANE), lambda b, g, i: (b, 0, NSA_BLK_VS + g)),
                  pl.BlockSpec((1, 1, t, LANE), lambda b, g, i: (b, g, i, 0)),
                  pl.BlockSpec((n, LANE, t), lambda b, g, i: (0, 0, 0))],
        out_specs=big,
        scratch_shapes=[pltpu.VMEM((NSA_HPG * t, LANE), BF16), pltpu.VMEM((NSA_HPG * t, LANE), F32),
                        pltpu.VMEM((NSA_HPG * t, LANE), F32), pltpu.VMEM((NSA_HPG * t, LANE), F32)],
        compiler_params=_cp(("parallel", "parallel", "arbitrary")),
        name="nsa_selected",
    )(z3, z3, z3, sel, _expand_blocks(S, t))


def _nsa_win_kernel(q_ref, k_ref, v_ref, oc_ref, os_ref, gl_ref, bg_ref, o_ref, q4_ref, m_ref, l_ref, acc_ref,
                    *, t, nw):
    qi = pl.program_id(2)
    _stack_heads(q_ref, q4_ref, t)
    row = lax.broadcasted_iota(jnp.int32, (t, t), 0)
    col = lax.broadcasted_iota(jnp.int32, (t, t), 1)

    def tile(w, mask, first):
        off = pl.multiple_of((qi - w) * t, t)
        s = lax.dot_general(q4_ref[...], k_ref[0, pl.ds(off, t), :], _NT, preferred_element_type=F32)
        _softmax_tile(_mask_heads(s, mask, t), v_ref[0, pl.ds(off, t), :], m_ref, l_ref, acc_ref, first)

    tile(0, col <= row, True)

    def body(w, carry):
        tile(w, col > row + (w * t - NSA_WINDOW), False)
        return carry

    lax.fori_loop(1, jnp.minimum(nw, qi + 1), body, 0)

    o_win = _softmax_finish(l_ref, acc_ref)
    gates = jax.nn.sigmoid(gl_ref[0].astype(F32) + bg_ref[0])
    for h in range(NSA_HPG):
        cols = slice(h * LANE, (h + 1) * LANE)
        o = (gates[:, h:h + 1] * oc_ref[0, :, cols].astype(F32)
             + gates[:, 4 + h:5 + h] * os_ref[0, :, cols].astype(F32)
             + gates[:, 8 + h:9 + h] * o_win[h * t:(h + 1) * t])
        o_ref[0, :, cols] = o.astype(o_ref.dtype)


def nsa_window_merge(z3, o_cmp, o_sel, bg, *, t=256):
    B, S, _ = z3.shape
    G = NSA_KV_GROUPS
    t = min(t, S)
    n = S // t
    assert NSA_WINDOW % t == 0
    nw = min(NSA_WINDOW // t + 1, n)
    big = pl.BlockSpec((1, t, 4 * LANE), lambda b, g, i: (b, i, g))
    return pl.pallas_call(
        functools.partial(_nsa_win_kernel, t=t, nw=nw),
        out_shape=jax.ShapeDtypeStruct((B, S, NSA_HEADS * NSA_HEAD_DIM), BF16),
        grid=(B, G, n),
        in_specs=[big,
                  pl.BlockSpec((1, S, LANE), lambda b, g, i: (b, 0, NSA_BLK_KW + g)),
                  pl.BlockSpec((1, S, LANE), lambda b, g, i: (b, 0, NSA_BLK_VW + g)),
                  big, big,
                  pl.BlockSpec((1, t, LANE), lambda b, g, i: (b, i, NSA_BLK_GATE + g)),
                  pl.BlockSpec((1, 1, LANE), lambda b, g, i: (g, 0, 0))],
        out_specs=big,
        scratch_shapes=[pltpu.VMEM((NSA_HPG * t, LANE), BF16), pltpu.VMEM((NSA_HPG * t, LANE), F32),
                        pltpu.VMEM((NSA_HPG * t, LANE), F32), pltpu.VMEM((NSA_HPG * t, LANE), F32)],
        compiler_params=_cp(("parallel", "parallel", "arbitrary")),
        name="nsa_window_merge",
    )(z3, z3, z3, o_cmp, o_sel, z3, bg)


def _rot_half_cols(w):
    half = w.shape[-1] // 2
    return jnp.concatenate([-w[..., half:], w[..., :half]], axis=-1)


def _prep_ab(w_in, w_alpha_up, b_alpha, w_uq, w_ukv):
    D = w_in.shape[0]
    q_g, k_g, v_g, g_g, a_lr, c_q, c_kv, k_r = _split_cols(w_in, AB_SPLITS)
    qk = jnp.concatenate([q_g.reshape(D, GLA_HEADS, GLA_DK), k_g.reshape(D, GLA_HEADS, GLA_DK)],
                         axis=-1).reshape(D, GLA_HEADS * LANE)
    tail = jnp.concatenate([k_r, _rot_half_cols(k_r), a_lr,
                            jnp.zeros((D, 512 - 2 * MLA_ROPE - GLA_GATE_RANK), w_in.dtype)], axis=-1)
    w = jnp.concatenate([qk, v_g, g_g, c_q, c_kv, tail], axis=-1).astype(BF16)
    wa = w_alpha_up.reshape(GLA_GATE_RANK, GLA_HEADS, GLA_DK).transpose(1, 0, 2)
    wa = jnp.concatenate([wa, wa], axis=-1)
    wa = jnp.pad(wa, ((0, 0), (0, LANE - GLA_GATE_RANK), (0, 0))).astype(BF16)
    ba = b_alpha.reshape(GLA_HEADS, 1, GLA_DK)
    ba = jnp.concatenate([ba, ba], axis=-1)
    wq = w_uq.reshape(MLA_Q_RANK, MLA_HEADS, MLA_QK)
    rope = wq[..., MLA_NOPE:]
    wq = jnp.concatenate([wq, _rot_half_cols(rope)], axis=-1).transpose(1, 0, 2).astype(BF16)
    wkv = w_ukv.reshape(MLA_KV_RANK, MLA_HEADS, MLA_NOPE + MLA_V).transpose(1, 0, 2).astype(BF16)
    return w, wa, ba, wq, wkv


def _prep_nsa(w_in, b_gate):
    D = w_in.shape[0]
    q, kc, vc, ks, vs, kw, vw, gl = _split_cols(w_in, NSA_SPLITS)
    G, HG = NSA_KV_GROUPS, NSA_HPG
    glp = gl.reshape(D, G, HG, 3).transpose(0, 1, 3, 2).reshape(D, G, 3 * HG)
    glp = jnp.pad(glp, ((0, 0), (0, 0), (0, LANE - 3 * HG))).reshape(D, G * LANE)
    w = jnp.concatenate([q, kc, ks, kw, vc, vs, vw, glp], axis=-1).astype(BF16)
    bg = b_gate.reshape(G, HG, 3).transpose(0, 2, 1).reshape(G, 1, 3 * HG)
    bg = jnp.pad(bg, ((0, 0), (0, 0), (0, LANE - 3 * HG)))
    return w, bg


def _overlap_t(S):
    nr = S // NSA_CMP_STRIDE
    ns = S // NSA_SEL_LEN
    c_start = np.arange(nr) * NSA_CMP_STRIDE
    c_end = c_start + NSA_CMP_LEN
    s_start = np.arange(ns) * NSA_SEL_LEN
    s_end = s_start + NSA_SEL_LEN
    ov = (c_start[None, :] < s_end[:, None]) & (c_end[None, :] > s_start[:, None])
    return jnp.asarray(ov.astype(np.float32), dtype=BF16)


def gla_mla_mixer(h, pre_w, B, S, c2, s2, w_in, w_alpha_up, b_alpha, gla_norm_w,
                  q_norm_w, w_uq, kv_norm_w, w_ukv):
    w, wa, ba, wq, wkv = _prep_ab(w_in, w_alpha_up, b_alpha, w_uq, w_ukv)
    z = norm_matmul(h, pre_w, w)
    z3 = z.reshape(B, S, AB_Z)
    o_gla = gla(z3, wa, ba, gla_norm_w)
    qm, km, vm = mla_proj(z, q_norm_w, kv_norm_w, wq, wkv, c2, s2, B, S)
    o_mla = mla_attention(qm, km, vm)
    return [o_gla.reshape(B * S, -1), o_mla.reshape(B * S, -1)]


def nsa_mixer(h, pre_w, B, S, c128, s128, w_in, b_gate, cmp_pos, cmp_w1, cmp_w2):
    w, bg = _prep_nsa(w_in, b_gate)
    z = norm_matmul_rope(h, pre_w, w, c128, s128)
    z3 = z.reshape(B, S, NSA_Z)
    G, dh = NSA_KV_GROUPS, NSA_HEAD_DIM

    def blocks(blk):
        t = z3[:, :, blk * LANE:(blk + G) * LANE].reshape(B, S, G, dh).transpose(0, 2, 1, 3)
        return t.reshape(B, G, S // NSA_CMP_STRIDE, NSA_CMP_STRIDE * dh)

    r = jnp.stack([blocks(NSA_BLK_KC), blocks(NSA_BLK_VC)])
    kv_cmp = nsa_compress(r, cmp_pos.reshape(2, 1, NSA_CMP_LEN * dh),
                          cmp_w1.astype(BF16), cmp_w2.astype(BF16))
    o_cmp, sel = nsa_cmp_select(z3, kv_cmp, _overlap_t(S))
    o_sel = nsa_selected(z3, sel)
    o = nsa_window_merge(z3, o_cmp, o_sel, bg)
    return [o.reshape(B * S, -1)]


def kernel(x, p, positions, ln_mix_pre, ln_mix_post, ln_ffn_pre, ln_ffn_post, ab_w_in, gla_w_alpha_up, gla_b_alpha, gla_norm_w, mla_q_norm_w, mla_w_uq, mla_kv_norm_w, mla_w_ukv, ab_w_out, nsa_w_in, nsa_b_gate, nsa_cmp_pos, nsa_cmp_w1, nsa_cmp_w2, nsa_w_out, ffn_w_gate, ffn_w_up, ffn_w_down, ple_w_gate, ple_b_gate, ple_w_proj):
    B, S, D = x.shape
    T = B * S
    depth = p.shape[0]
    c2, s2, c128, s128 = rope_tables(positions)
    h = x.reshape(T, D)
    ab_out, nsa_out = ab_w_out.astype(BF16), nsa_w_out.astype(BF16)
    w_gate, w_up, w_down = ffn_w_gate.astype(BF16), ffn_w_up.astype(BF16), ffn_w_down.astype(BF16)
    ple_gate, ple_proj = ple_w_gate.astype(BF16), ple_w_proj.astype(BF16)
    p3 = p.reshape(depth, T, -1)
    for i in range(depth):
        j = i // 2
        if i % 2 == 0:
            mix = gla_mla_mixer(h, ln_mix_pre[i], B, S, c2, s2, ab_w_in[j], gla_w_alpha_up[j],
                                gla_b_alpha[j], gla_norm_w[j], mla_q_norm_w[j], mla_w_uq[j],
                                mla_kv_norm_w[j], mla_w_ukv[j])
            w_out = ab_out
        else:
            mix = nsa_mixer(h, ln_mix_pre[i], B, S, c128, s128, nsa_w_in[j], nsa_b_gate[j],
                            nsa_cmp_pos[j], nsa_cmp_w1[j], nsa_cmp_w2[j])
            w_out = nsa_out
        h = matmul_parts_norm_residual(mix, w_out, j, h, ln_mix_post[i])
        act = norm_swiglu(h, ln_ffn_pre[i], w_gate, w_up, i)
        h = matmul_norm_residual(act, w_down, i, h, ln_ffn_post[i], tk=FFN_HIDDEN // 4)
        h = ple(h, ple_gate, ple_b_gate[i], p3, ple_proj, i)
    return h.reshape(B, S, D)
```

```python
import functools

import numpy as np
import jax
import jax.numpy as jnp
from jax import lax
from jax.experimental import pallas as pl
from jax.experimental.pallas import tpu as pltpu

F32 = jnp.float32
BF16 = jnp.bfloat16

D_MODEL = 2048
PLE_DIM = 256
ROPE_THETA = 10000.0
NORM_EPS = 1e-6
NEG = -1e30
FORCE = 1e6

GLA_HEADS = 8
GLA_DK = 64
GLA_DV = 128
GLA_GATE_RANK = 16
GLA_TAU = 16.0
GLA_CHUNK = 64

MLA_HEADS = 8
MLA_Q_RANK = 512
MLA_KV_RANK = 512
MLA_NOPE = 128
MLA_ROPE = 64
MLA_V = 128
MLA_QK = MLA_NOPE + MLA_ROPE

NSA_HEADS = 16
NSA_KV_GROUPS = 4
NSA_HPG = NSA_HEADS // NSA_KV_GROUPS
NSA_HEAD_DIM = 128
NSA_CMP_LEN = 32
NSA_CMP_STRIDE = 16
NSA_SEL_LEN = 64
NSA_SEL_TOPK = 16
NSA_WINDOW = 512

FFN_HIDDEN = 5632

AB_SPLITS = (GLA_HEADS * GLA_DK, GLA_HEADS * GLA_DK, GLA_HEADS * GLA_DV, GLA_HEADS * GLA_DV,
             GLA_GATE_RANK, MLA_Q_RANK, MLA_KV_RANK, MLA_ROPE)
NSA_KV_W = NSA_KV_GROUPS * NSA_HEAD_DIM
NSA_SPLITS = (NSA_HEADS * NSA_HEAD_DIM,) + (NSA_KV_W,) * 6 + (3 * NSA_HEADS,)

LOG2E = 1.4426950408889634
LANE = 128
VMEM_LIMIT = 56 * 1024 * 1024

AB_Z = 4608
AB_BLK_QK = 0
AB_BLK_V = 8
AB_BLK_G = 16
AB_BLK_CQ = 24
AB_BLK_CKV = 28
AB_BLK_KR = 32
AB_BLK_ALR = 33

NSA_Z = 5632
NSA_BLK_Q = 0
NSA_BLK_KC = 16
NSA_BLK_KS = 20
NSA_BLK_KW = 24
NSA_BLK_VC = 28
NSA_BLK_VS = 32
NSA_BLK_VW = 36
NSA_BLK_GATE = 40
NSA_ROPE_TILES = 7
NSA_Q_TILES = 4


def _cp(sem):
    return pltpu.CompilerParams(dimension_semantics=sem, vmem_limit_bytes=VMEM_LIMIT)


def _rms(x, w):
    return x * lax.rsqrt(jnp.mean(x * x, axis=-1, keepdims=True) + NORM_EPS) * w


def _split_cols(z, widths):
    out, off = [], 0
    for w in widths:
        out.append(z[..., off:off + w])
        off += w
    return out


def _tables_kernel(pos_ref, inv_ref, c2_ref, s2_ref, c128_ref, s128_ref):
    pos = pos_ref[...].astype(F32)
    lane = lax.broadcasted_iota(jnp.int32, (1, LANE), 1)
    lo = lane < 64
    a64 = pos * inv_ref[0:1, :]
    c2_ref[...] = jnp.where(lo, jnp.cos(a64), 0.0)
    s2_ref[...] = jnp.where(lo, jnp.sin(a64), 0.0)
    a128 = pos * inv_ref[1:2, :]
    s = jnp.sin(a128)
    c128_ref[...] = jnp.cos(a128)
    s128_ref[...] = jnp.where(lo, -s, s)


def rope_tables(positions):
    T = positions.size
    inv32 = jnp.power(ROPE_THETA, -jnp.arange(0, MLA_ROPE, 2, dtype=F32) / MLA_ROPE)
    inv64 = jnp.power(ROPE_THETA, -jnp.arange(0, NSA_HEAD_DIM, 2, dtype=F32) / NSA_HEAD_DIM)
    inv = jnp.zeros((8, LANE), F32)
    inv = inv.at[0, :64].set(jnp.concatenate([inv32, inv32]))
    inv = inv.at[1, :].set(jnp.concatenate([inv64, inv64]))
    tm = min(T, 1024)
    spec = pl.BlockSpec((tm, LANE), lambda i: (i, 0))
    return pl.pallas_call(
        _tables_kernel,
        out_shape=[jax.ShapeDtypeStruct((T, LANE), F32)] * 4,
        grid=(T // tm,),
        in_specs=[pl.BlockSpec((tm, 1), lambda i: (i, 0)), pl.BlockSpec((8, LANE), lambda i: (0, 0))],
        out_specs=[spec] * 4,
        compiler_params=_cp(("parallel",)),
        name="rope_tables",
    )(positions.reshape(T, 1), inv)


def _norm_mm_kernel(x_ref, nw_ref, w_ref, o_ref, xn_ref):
    @pl.when(pl.program_id(1) == 0)
    def _():
        xn_ref[...] = _rms(x_ref[...], nw_ref[...]).astype(BF16)

    o_ref[...] = jnp.dot(xn_ref[...], w_ref[...], preferred_element_type=F32).astype(o_ref.dtype)


def _norm_mm_rope_kernel(x_ref, nw_ref, w_ref, cos_ref, sin_ref, o_ref, xn_ref, *, tn, scale):
    j = pl.program_id(1)

    @pl.when(j == 0)
    def _():
        xn_ref[...] = _rms(x_ref[...], nw_ref[...]).astype(BF16)

    y = jnp.dot(xn_ref[...], w_ref[...], preferred_element_type=F32)
    mult = jnp.where(j < NSA_Q_TILES, scale, 1.0)
    is_rope = j < NSA_ROPE_TILES
    cos = jnp.where(is_rope, cos_ref[...] * mult, 1.0)
    sin = jnp.where(is_rope, sin_ref[...] * mult, 0.0)
    for c in range(tn // LANE):
        seg = y[:, c * LANE:(c + 1) * LANE]
        o_ref[:, c * LANE:(c + 1) * LANE] = (seg * cos + pltpu.roll(seg, 64, 1) * sin).astype(o_ref.dtype)


def _norm_swiglu_kernel(x_ref, nw_ref, wg_ref, wu_ref, o_ref, xn_ref):
    @pl.when(pl.program_id(1) == 0)
    def _():
        xn_ref[...] = _rms(x_ref[...], nw_ref[...]).astype(BF16)

    xn = xn_ref[...]
    g = jnp.dot(xn, wg_ref[...], preferred_element_type=F32)
    u = jnp.dot(xn, wu_ref[...], preferred_element_type=F32)
    o_ref[...] = (g * jax.nn.sigmoid(g) * u).astype(o_ref.dtype)


def _row_tile(T, want):
    return min(T, want)


def norm_matmul(x, nw, w, *, tm=1024, tn=512):
    T, D = x.shape
    N = w.shape[1]
    tm = _row_tile(T, tm)
    return pl.pallas_call(
        _norm_mm_kernel,
        out_shape=jax.ShapeDtypeStruct((T, N), BF16),
        grid=(T // tm, N // tn),
        in_specs=[pl.BlockSpec((tm, D), lambda i, j: (i, 0)),
                  pl.BlockSpec((1, D), lambda i, j: (0, 0)),
                  pl.BlockSpec((D, tn), lambda i, j: (0, j))],
        out_specs=pl.BlockSpec((tm, tn), lambda i, j: (i, j)),
        scratch_shapes=[pltpu.VMEM((tm, D), BF16)],
        compiler_params=_cp(("parallel", "arbitrary")),
        name="norm_matmul",
    )(x, nw.reshape(1, D), w)


def norm_matmul_rope(x, nw, w, cos, sin, *, tm=1024, tn=512):
    T, D = x.shape
    N = w.shape[1]
    tm = _row_tile(T, tm)
    kern = functools.partial(_norm_mm_rope_kernel, tn=tn, scale=NSA_HEAD_DIM ** -0.5 * LOG2E)
    return pl.pallas_call(
        kern,
        out_shape=jax.ShapeDtypeStruct((T, N), BF16),
        grid=(T // tm, N // tn),
        in_specs=[pl.BlockSpec((tm, D), lambda i, j: (i, 0)),
                  pl.BlockSpec((1, D), lambda i, j: (0, 0)),
                  pl.BlockSpec((D, tn), lambda i, j: (0, j)),
                  pl.BlockSpec((tm, LANE), lambda i, j: (i, 0)),
                  pl.BlockSpec((tm, LANE), lambda i, j: (i, 0))],
        out_specs=pl.BlockSpec((tm, tn), lambda i, j: (i, j)),
        scratch_shapes=[pltpu.VMEM((tm, D), BF16)],
        compiler_params=_cp(("parallel", "arbitrary")),
        name="norm_matmul_rope",
    )(x, nw.reshape(1, D), w, cos, sin)


def norm_swiglu(x, nw, wg, wu, layer, *, tm=1024, tn=512):
    T, D = x.shape
    N = wg.shape[2]
    tm = _row_tile(T, tm)
    wspec = pl.BlockSpec((None, D, tn), lambda i, j: (layer, 0, j))
    return pl.pallas_call(
        _norm_swiglu_kernel,
        out_shape=jax.ShapeDtypeStruct((T, N), BF16),
        grid=(T // tm, N // tn),
        in_specs=[pl.BlockSpec((tm, D), lambda i, j: (i, 0)),
                  pl.BlockSpec((1, D), lambda i, j: (0, 0)),
                  wspec, wspec],
        out_specs=pl.BlockSpec((tm, tn), lambda i, j: (i, j)),
        scratch_shapes=[pltpu.VMEM((tm, D), BF16)],
        compiler_params=_cp(("parallel", "arbitrary")),
        name="norm_swiglu",
    )(x, nw.reshape(1, D), wg, wu)


def _mm_norm_res_kernel(a_ref, w_ref, h_ref, nw_ref, o_ref, acc_ref):
    k = pl.program_id(1)

    @pl.when(k == 0)
    def _():
        acc_ref[...] = jnp.zeros_like(acc_ref)

    acc_ref[...] += jnp.dot(a_ref[...], w_ref[...], preferred_element_type=F32)

    @pl.when(k == pl.num_programs(1) - 1)
    def _():
        o_ref[...] = h_ref[...] + _rms(acc_ref[...], nw_ref[...])


def _mm_norm_res_parts_kernel(*refs, widths):
    a_refs = refs[:len(widths)]
    w_ref, h_ref, nw_ref, o_ref = refs[len(widths):]
    m, off = None, 0
    for a_ref, wd in zip(a_refs, widths):
        part = jnp.dot(a_ref[...], w_ref[off:off + wd, :], preferred_element_type=F32)
        m = part if m is None else m + part
        off += wd
    o_ref[...] = h_ref[...] + _rms(m, nw_ref[...])


def matmul_parts_norm_residual(parts, w, layer, h, nw, *, tm=512):
    T = parts[0].shape[0]
    widths = tuple(a.shape[1] for a in parts)
    K, D = w.shape[1:]
    assert sum(widths) == K
    tm = _row_tile(T, tm)
    return pl.pallas_call(
        functools.partial(_mm_norm_res_parts_kernel, widths=widths),
        out_shape=jax.ShapeDtypeStruct((T, D), F32),
        grid=(T // tm,),
        in_specs=[pl.BlockSpec((tm, wd), lambda i: (i, 0)) for wd in widths]
        + [pl.BlockSpec((None, K, D), lambda i: (layer, 0, 0)),
           pl.BlockSpec((tm, D), lambda i: (i, 0)),
           pl.BlockSpec((1, D), lambda i: (0, 0))],
        out_specs=pl.BlockSpec((tm, D), lambda i: (i, 0)),
        compiler_params=_cp(("parallel",)),
        name="matmul_parts_norm_residual",
    )(*parts, w, h, nw.reshape(1, D))


def matmul_norm_residual(a, w, layer, h, nw, *, tm=512, tk=512):
    T, K = a.shape
    D = w.shape[2]
    tm = _row_tile(T, tm)
    return pl.pallas_call(
        _mm_norm_res_kernel,
        out_shape=jax.ShapeDtypeStruct((T, D), F32),
        grid=(T // tm, K // tk),
        in_specs=[pl.BlockSpec((tm, tk), lambda i, k: (i, k)),
                  pl.BlockSpec((None, tk, D), lambda i, k: (layer, k, 0)),
                  pl.BlockSpec((tm, D), lambda i, k: (i, 0)),
                  pl.BlockSpec((1, D), lambda i, k: (0, 0))],
        out_specs=pl.BlockSpec((tm, D), lambda i, k: (i, 0)),
        scratch_shapes=[pltpu.VMEM((tm, D), F32)],
        compiler_params=_cp(("parallel", "arbitrary")),
        name="matmul_norm_residual",
    )(a, w, h, nw.reshape(1, D))


def _ple_kernel(h_ref, wg_ref, bg_ref, p_ref, wp_ref, o_ref, hb_ref, *, tn):
    j = pl.program_id(1)

    @pl.when(j == 0)
    def _():
        hb_ref[...] = h_ref[...].astype(BF16)

    g = jnp.dot(hb_ref[...], wg_ref[...], preferred_element_type=F32) + bg_ref[...]
    pp = jnp.dot(p_ref[...].astype(BF16), wp_ref[...], preferred_element_type=F32)
    hs = h_ref[:, pl.ds(pl.multiple_of(j * tn, tn), tn)]
    o_ref[...] = hs + jax.nn.sigmoid(g) * pp


def ple(h, wg, bg, p, wp, layer, *, tm=1024, tn=512):
    T, D = h.shape
    P = p.shape[2]
    tm = _row_tile(T, tm)
    return pl.pallas_call(
        functools.partial(_ple_kernel, tn=tn),
        out_shape=jax.ShapeDtypeStruct((T, D), F32),
        grid=(T // tm, D // tn),
        in_specs=[pl.BlockSpec((tm, D), lambda i, j: (i, 0)),
                  pl.BlockSpec((None, D, tn), lambda i, j: (layer, 0, j)),
                  pl.BlockSpec((1, tn), lambda i, j: (0, j)),
                  pl.BlockSpec((None, tm, P), lambda i, j: (layer, i, 0)),
                  pl.BlockSpec((None, P, tn), lambda i, j: (layer, 0, j))],
        out_specs=pl.BlockSpec((tm, tn), lambda i, j: (i, j)),
        scratch_shapes=[pltpu.VMEM((tm, D), BF16)],
        compiler_params=_cp(("parallel", "arbitrary")),
        name="ple",
    )(h, wg, bg.reshape(1, D), p, wp)


def _split3(x):
    h1 = x.astype(BF16)
    r1 = x - h1.astype(F32)
    h2 = r1.astype(BF16)
    h3 = (r1 - h2.astype(F32)).astype(BF16)
    return h1, h2, h3


def _gla_kernel(qk_ref, v_ref, g_ref, alr_ref, wa_ref, ba_ref, nw_ref, o_ref, st_ref, *, tr):
    L = GLA_CHUNK

    @pl.when(pl.program_id(2) == 0)
    def _():
        st_ref[...] = jnp.zeros_like(st_ref)

    x = jnp.dot(alr_ref[0], wa_ref[0], preferred_element_type=F32) + ba_ref[0]
    log_a = (jnp.minimum(x, 0.0) - jnp.log1p(jnp.exp(-jnp.abs(x)))) * (1.0 / GLA_TAU)

    row = lax.broadcasted_iota(jnp.int32, (tr, tr), 0)
    col = lax.broadcasted_iota(jnp.int32, (tr, tr), 1)
    same = lax.shift_right_logical(row, 6) == lax.shift_right_logical(col, 6)
    causal = same & (row >= col)
    tri = jnp.where(causal, 1.0, 0.0).astype(BF16)
    ones = jnp.where(same, 1.0, 0.0).astype(BF16)
    lo = lax.broadcasted_iota(jnp.int32, (1, LANE), 1) < GLA_DK
    tn = (((0,), (0,)), ((), ()))

    parts = _split3(log_a)
    b = functools.reduce(jnp.add, [jnp.dot(tri, a, preferred_element_type=F32) for a in parts])
    b_end = functools.reduce(jnp.add, [jnp.dot(ones, a, preferred_element_type=F32) for a in parts])
    blk = qk_ref[0].astype(F32)
    swp = pltpu.roll(blk, 64, 1)
    q_dec = jnp.where(lo, blk * jnp.exp(b) * (GLA_DK ** -0.5), 0.0).astype(BF16)
    k_inv = jnp.where(lo, swp * jnp.exp(-b), 0.0).astype(BF16)
    k_end = jnp.where(lo, swp * jnp.exp(b_end - b), 0.0).astype(BF16)
    decay = jnp.exp(b_end)
    v = v_ref[0]
    attn = lax.dot_general(q_dec, k_inv, _NT, preferred_element_type=F32)
    attn = jnp.where(causal, attn, 0.0).astype(BF16)
    o_intra = jnp.dot(attn, v, preferred_element_type=F32)

    st = st_ref[...]
    outs = []
    for c in range(tr // L):
        rows = slice(c * L, (c + 1) * L)
        outs.append(o_intra[rows] + lax.dot_general(q_dec[rows], st.astype(BF16), _NT,
                                                    preferred_element_type=F32))
        st = st * decay[c * L:c * L + 1] + lax.dot_general(v[rows], k_end[rows], tn,
                                                           preferred_element_type=F32)
    st_ref[...] = st
    o = jnp.concatenate(outs, axis=0)
    g = g_ref[0].astype(F32)
    o_ref[0] = (_rms(o, nw_ref[...]) * (g * jax.nn.sigmoid(g))).astype(o_ref.dtype)


def gla(z3, wa, ba, nw, *, tr=256):
    B, S, _ = z3.shape
    tr = min(tr, S)
    H = GLA_HEADS
    return pl.pallas_call(
        functools.partial(_gla_kernel, tr=tr),
        out_shape=jax.ShapeDtypeStruct((B, S, H * GLA_DV), BF16),
        grid=(B, H, S // tr),
        in_specs=[pl.BlockSpec((1, tr, LANE), lambda b, h, r: (b, r, AB_BLK_QK + h)),
                  pl.BlockSpec((1, tr, LANE), lambda b, h, r: (b, r, AB_BLK_V + h)),
                  pl.BlockSpec((1, tr, LANE), lambda b, h, r: (b, r, AB_BLK_G + h)),
                  pl.BlockSpec((1, tr, LANE), lambda b, h, r: (b, r, AB_BLK_ALR)),
                  pl.BlockSpec((1, LANE, LANE), lambda b, h, r: (h, 0, 0)),
                  pl.BlockSpec((1, 1, LANE), lambda b, h, r: (h, 0, 0)),
                  pl.BlockSpec((1, LANE), lambda b, h, r: (0, 0))],
        out_specs=pl.BlockSpec((1, tr, LANE), lambda b, h, r: (b, r, h)),
        scratch_shapes=[pltpu.VMEM((GLA_DV, LANE), F32)],
        compiler_params=_cp(("parallel", "parallel", "arbitrary")),
        name="gla",
    )(z3, z3, z3, z3, wa, ba, nw.reshape(1, GLA_DV))


def _mla_proj_kernel(cq_ref, ckv_ref, kr_ref, qnw_ref, kvnw_ref, wq_ref, wkv_ref, c2_ref, s2_ref,
                     q_ref, k_ref, v_ref, cqn_ref, ckvn_ref, kro_ref):
    @pl.when(pl.program_id(1) == 0)
    def _():
        cqn_ref[...] = _rms(cq_ref[...].astype(F32), qnw_ref[...]).astype(BF16)
        ckvn_ref[...] = _rms(ckv_ref[...].astype(F32), kvnw_ref[...]).astype(BF16)
        kr = kr_ref[...].astype(F32)
        kro_ref[...] = (kr * c2_ref[...] + pltpu.roll(kr, 64, 1) * s2_ref[...]).astype(BF16)

    scale = MLA_QK ** -0.5 * LOG2E
    yq = jnp.dot(cqn_ref[...], wq_ref[0], preferred_element_type=F32)
    y2 = yq[:, LANE:]
    qr = y2 * c2_ref[...] + pltpu.roll(y2, 64, 1) * s2_ref[...]
    q_ref[0, 0, :, :MLA_NOPE] = (yq[:, :LANE] * scale).astype(BF16)
    q_ref[0, 0, :, MLA_NOPE:] = (qr[:, :MLA_ROPE] * scale).astype(BF16)
    ykv = jnp.dot(ckvn_ref[...], wkv_ref[0], preferred_element_type=F32)
    k_ref[0, 0, :, :MLA_NOPE] = ykv[:, :LANE].astype(BF16)
    k_ref[0, 0, :, MLA_NOPE:] = kro_ref[:, :MLA_ROPE]
    v_ref[0, 0] = ykv[:, LANE:].astype(BF16)


def mla_proj(z, qnw, kvnw, wq, wkv, c2, s2, B, S, *, tm=512):
    T = z.shape[0]
    tm = min(tm, S)
    H = MLA_HEADS
    nb = S // tm

    def omap(i, h):
        return (i // nb, h, i % nb, 0)

    return pl.pallas_call(
        _mla_proj_kernel,
        out_shape=[jax.ShapeDtypeStruct((B, H, S, MLA_QK), BF16),
                   jax.ShapeDtypeStruct((B, H, S, MLA_QK), BF16),
                   jax.ShapeDtypeStruct((B, H, S, MLA_V), BF16)],
        grid=(T // tm, H),
        in_specs=[pl.BlockSpec((tm, MLA_Q_RANK), lambda i, h: (i, AB_BLK_CQ // 4)),
                  pl.BlockSpec((tm, MLA_KV_RANK), lambda i, h: (i, AB_BLK_CKV // 4)),
                  pl.BlockSpec((tm, LANE), lambda i, h: (i, AB_BLK_KR)),
                  pl.BlockSpec((1, MLA_Q_RANK), lambda i, h: (0, 0)),
                  pl.BlockSpec((1, MLA_KV_RANK), lambda i, h: (0, 0)),
                  pl.BlockSpec((1, MLA_Q_RANK, 2 * LANE), lambda i, h: (h, 0, 0)),
                  pl.BlockSpec((1, MLA_KV_RANK, 2 * LANE), lambda i, h: (h, 0, 0)),
                  pl.BlockSpec((tm, LANE), lambda i, h: (i, 0)),
                  pl.BlockSpec((tm, LANE), lambda i, h: (i, 0))],
        out_specs=[pl.BlockSpec((1, 1, tm, MLA_QK), omap),
                   pl.BlockSpec((1, 1, tm, MLA_QK), omap),
                   pl.BlockSpec((1, 1, tm, MLA_V), omap)],
        scratch_shapes=[pltpu.VMEM((tm, MLA_Q_RANK), BF16), pltpu.VMEM((tm, MLA_KV_RANK), BF16),
                        pltpu.VMEM((tm, LANE), BF16)],
        compiler_params=_cp(("parallel", "arbitrary")),
        name="mla_proj",
    )(z, z, z, qnw.reshape(1, -1), kvnw.reshape(1, -1), wq, wkv, c2, s2)


_NT = (((1,), (1,)), ((), ()))


def _softmax_tile(s, v, m_ref, l_ref, acc_ref, first):
    M, tk = s.shape
    chunks = [s[:, c * LANE:(c + 1) * LANE] for c in range(tk // LANE)]
    mrow = jnp.max(functools.reduce(jnp.maximum, chunks), axis=-1, keepdims=True)
    if first:
        m_new = jnp.broadcast_to(mrow, (M, LANE))
    else:
        m_prev = m_ref[...]
        m_new = jnp.maximum(m_prev, mrow)
    ps = [jnp.exp2(c - m_new) for c in chunks]
    lsum = functools.reduce(jnp.add, ps)
    p = (jnp.concatenate(ps, axis=1) if len(ps) > 1 else ps[0]).astype(BF16)
    pv = jnp.dot(p, v, preferred_element_type=F32)
    if first:
        l_ref[...] = lsum
        acc_ref[...] = pv
    else:
        alpha = jnp.exp2(m_prev - m_new)
        l_ref[...] = alpha * l_ref[...] + lsum
        acc_ref[...] = alpha * acc_ref[...] + pv
    m_ref[...] = m_new


def _softmax_finish(l_ref, acc_ref):
    return acc_ref[...] / jnp.sum(l_ref[...], axis=-1, keepdims=True)


def _mla_attn_kernel(q_ref, k_ref, v_ref, o_ref, m_ref, l_ref, acc_ref, *, tq, tk):
    qi = pl.program_id(2)
    r = tq // tk
    q = q_ref[0, 0]
    row = lax.broadcasted_iota(jnp.int32, (tq, tk), 0)
    col = lax.broadcasted_iota(jnp.int32, (tq, tk), 1)

    def score(j, d):
        off = pl.multiple_of(j * tk, tk)
        s = lax.dot_general(q, k_ref[0, 0, pl.ds(off, tk), :], _NT, preferred_element_type=F32)
        if d is not None:
            s = jnp.where(col + d * tk <= row, s, NEG)
        return s, v_ref[0, 0, pl.ds(off, tk), :]

    def attend(tiles, first):
        for n, (s, v) in enumerate(tiles):
            _softmax_tile(s, v, m_ref, l_ref, acc_ref, first and n == 0)

    attend([score(qi * r + d, d) for d in range(r)], True)

    def body(i, carry):
        attend([score(2 * i, None), score(2 * i + 1, None)], False)
        return carry

    lax.fori_loop(0, qi * (r // 2), body, 0)
    o_ref[0] = _softmax_finish(l_ref, acc_ref).astype(o_ref.dtype)


def mla_attention(q, k, v, *, tq=512, tk=256):
    B, H, S, _ = q.shape
    tq = min(tq, S)
    tk = min(tk, tq // 2)
    assert tq % (2 * tk) == 0
    return pl.pallas_call(
        functools.partial(_mla_attn_kernel, tq=tq, tk=tk),
        out_shape=jax.ShapeDtypeStruct((B, S, H * MLA_V), BF16),
        grid=(B, H, S // tq),
        in_specs=[pl.BlockSpec((1, 1, tq, MLA_QK), lambda b, h, i: (b, h, i, 0)),
                  pl.BlockSpec((1, 1, S, MLA_QK), lambda b, h, i: (b, h, 0, 0)),
                  pl.BlockSpec((1, 1, S, MLA_V), lambda b, h, i: (b, h, 0, 0))],
        out_specs=pl.BlockSpec((1, tq, MLA_V), lambda b, h, i: (b, i, h)),
        scratch_shapes=[pltpu.VMEM((tq, LANE), F32), pltpu.VMEM((tq, LANE), F32),
                        pltpu.VMEM((tq, MLA_V), F32)],
        compiler_params=_cp(("parallel", "parallel", "arbitrary")),
        name="mla_attention",
    )(q, k, v)


def _nsa_compress_kernel(r_ref, pos_ref, w1_ref, w2_ref, o_ref):
    r = r_ref[0, 0, 0]
    half = r.shape[1]
    nr = r.shape[0]
    a = jnp.dot(r, w1_ref[0, :half, :], preferred_element_type=F32)
    b = jnp.dot(r, w1_ref[0, half:, :], preferred_element_type=F32)
    pos = jnp.broadcast_to(pos_ref[0], (8, 2 * half)).astype(BF16)
    c = jnp.dot(pos, w1_ref[0], preferred_element_type=F32)[0:1, :]
    pre = a + pltpu.roll(b, nr - 1, 0) + c
    o_ref[0, 0, 0] = jnp.dot(jax.nn.gelu(pre).astype(BF16), w2_ref[0],
                             preferred_element_type=F32).astype(o_ref.dtype)


def nsa_compress(r, pos, w1, w2):
    _, B, G, NR, W = r.shape
    dh = NSA_HEAD_DIM
    return pl.pallas_call(
        _nsa_compress_kernel,
        out_shape=jax.ShapeDtypeStruct((2, B, G, NR, dh), BF16),
        grid=(2, B, G),
        in_specs=[pl.BlockSpec((1, 1, 1, NR, W), lambda c, b, g: (c, b, g, 0, 0)),
                  pl.BlockSpec((1, 1, 2 * W), lambda c, b, g: (c, 0, 0)),
                  pl.BlockSpec((1, 2 * W, dh), lambda c, b, g: (c, 0, 0)),
                  pl.BlockSpec((1, dh, dh), lambda c, b, g: (c, 0, 0))],
        out_specs=pl.BlockSpec((1, 1, 1, NR, dh), lambda c, b, g: (c, b, g, 0, 0)),
        compiler_params=_cp(("parallel", "parallel", "parallel")),
        name="nsa_compress",
    )(r, pos, w1, w2)


def _nsa_cmp_kernel(q_ref, kc_ref, vc_ref, ovt_ref, o_ref, sel_ref, *, tq, ns, ksel):
    qi = pl.program_id(2)
    nc = kc_ref.shape[3]
    kc = kc_ref[0, 0, 0]
    vc = vc_ref[0, 0, 0]
    t = qi * tq + lax.broadcasted_iota(jnp.int32, (tq, 1), 0)
    n = lax.broadcasted_iota(jnp.int32, (1, nc), 1)
    ok = (n * NSA_CMP_STRIDE + (NSA_CMP_LEN - 1)) <= t
    nt = (((1,), (1,)), ((), ()))
    psum = jnp.zeros((tq, nc), F32)
    for j in range(NSA_HPG):
        q = q_ref[0, :, j * LANE:(j + 1) * LANE]
        s = lax.dot_general(q, kc, nt, preferred_element_type=F32)
        s = jnp.where(ok, s, NEG)
        e = jnp.where(ok, jnp.exp2(s - jnp.max(s, axis=-1, keepdims=True)), 0.0)
        d = jnp.sum(e, axis=-1, keepdims=True)
        p = e * jnp.where(d > 0.0, 1.0 / d, 0.0)
        o_ref[0, :, j * LANE:(j + 1) * LANE] = jnp.dot(p.astype(BF16), vc,
                                                        preferred_element_type=F32).astype(o_ref.dtype)
        psum = psum + p
    ph = psum.astype(BF16)
    plo = (psum - ph.astype(F32)).astype(BF16)
    ovt = ovt_ref[...]
    imp = (lax.dot_general(ovt, ph, nt, preferred_element_type=F32)
           + lax.dot_general(ovt, plo, nt, preferred_element_type=F32))
    m = lax.broadcasted_iota(jnp.int32, (ns, 1), 0)
    tt = qi * tq + lax.broadcasted_iota(jnp.int32, (1, tq), 1)
    causal = m * NSA_SEL_LEN <= tt
    cur = lax.shift_right_logical(tt, 6)
    forced = (m == 0) | (m == cur) | (m == cur - 1)
    score = jnp.where(causal, jnp.where(forced, FORCE, imp), -FORCE)
    cnt = jnp.zeros((ns, tq), F32)
    for m2 in range(ns):
        r = score[m2:m2 + 1, :]
        ahead = (r > score) | ((r == score) & (m2 < m))
        cnt = cnt + jnp.where(ahead, 1.0, 0.0)
    sel = jnp.where((cnt < float(ksel)) & causal, 1.0, 0.0)
    selp = jnp.concatenate([sel, jnp.zeros((LANE - ns, tq), F32)], axis=0)
    sel_ref[0, 0] = selp.T.astype(sel_ref.dtype)


def nsa_cmp_select(z3, kv_cmp, ovt, *, tq=512):
    B, S, _ = z3.shape
    G = NSA_KV_GROUPS
    tq = min(tq, S)
    ns = S // NSA_SEL_LEN
    nc = kv_cmp.shape[3]
    kern = functools.partial(_nsa_cmp_kernel, tq=tq, ns=ns, ksel=min(NSA_SEL_TOPK, ns))
    return pl.pallas_call(
        kern,
        out_shape=[jax.ShapeDtypeStruct((B, S, NSA_HEADS * NSA_HEAD_DIM), BF16),
                   jax.ShapeDtypeStruct((B, G, S, LANE), BF16)],
        grid=(B, G, S // tq),
        in_specs=[pl.BlockSpec((1, tq, 4 * LANE), lambda b, g, i: (b, i, g)),
                  pl.BlockSpec((1, 1, 1, nc, LANE), lambda b, g, i: (0, b, g, 0, 0)),
                  pl.BlockSpec((1, 1, 1, nc, LANE), lambda b, g, i: (1, b, g, 0, 0)),
                  pl.BlockSpec((ns, nc), lambda b, g, i: (0, 0))],
        out_specs=[pl.BlockSpec((1, tq, 4 * LANE), lambda b, g, i: (b, i, g)),
                   pl.BlockSpec((1, 1, tq, LANE), lambda b, g, i: (b, g, i, 0))],
        compiler_params=_cp(("parallel", "parallel", "parallel")),
        name="nsa_cmp_select",
    )(z3, kv_cmp, kv_cmp, ovt)


def _stack_heads(q_ref, q4_ref, t):
    for h in range(NSA_HPG):
        q4_ref[h * t:(h + 1) * t, :] = q_ref[0, :, h * LANE:(h + 1) * LANE]


def _mask_heads(s, mask, t):
    return jnp.concatenate([jnp.where(mask, s[h * t:(h + 1) * t], NEG) for h in range(NSA_HPG)], axis=0)


def _nsa_sel_kernel(q_ref, k_ref, v_ref, sel_ref, e_ref, o_ref, q4_ref, m_ref, l_ref, acc_ref, *, t):
    qi = pl.program_id(2)
    _stack_heads(q_ref, q4_ref, t)
    sel = sel_ref[0, 0]
    row = lax.broadcasted_iota(jnp.int32, (t, t), 0)
    col = lax.broadcasted_iota(jnp.int32, (t, t), 1)

    def score(j, diag):
        off = pl.multiple_of(j * t, t)
        mask = jnp.dot(sel, e_ref[j], preferred_element_type=F32) > 0.5
        if diag:
            mask = mask & (col <= row)
        s = lax.dot_general(q4_ref[...], k_ref[0, pl.ds(off, t), :], _NT, preferred_element_type=F32)
        return _mask_heads(s, mask, t), v_ref[0, pl.ds(off, t), :]

    def attend(tiles, first):
        for n, (s, v) in enumerate(tiles):
            _softmax_tile(s, v, m_ref, l_ref, acc_ref, first and n == 0)

    attend([score(qi, True)], True)
    odd = qi & 1

    @pl.when(odd == 1)
    def _():
        attend([score(0, False)], False)

    def body(i, carry):
        j = odd + 2 * i
        attend([score(j, False), score(j + 1, False)], False)
        return carry

    lax.fori_loop(0, lax.shift_right_logical(qi, 1), body, 0)
    o = _softmax_finish(l_ref, acc_ref)
    for h in range(NSA_HPG):
        o_ref[0, :, h * LANE:(h + 1) * LANE] = o[h * t:(h + 1) * t].astype(o_ref.dtype)


def _expand_blocks(S, t):
    key = np.arange(S).reshape(S // t, 1, t)
    blk = np.arange(LANE).reshape(1, LANE, 1)
    return jnp.asarray((key // NSA_SEL_LEN == blk).astype(np.float32), dtype=BF16)


def nsa_selected(z3, sel, *, t=256):
    B, S, _ = z3.shape
    G = NSA_KV_GROUPS
    t = min(t, S)
    n = S // t
    big = pl.BlockSpec((1, t, 4 * LANE), lambda b, g, i: (b, i, g))
    return pl.pallas_call(
        functools.partial(_nsa_sel_kernel, t=t),
        out_shape=jax.ShapeDtypeStruct((B, S, NSA_HEADS * NSA_HEAD_DIM), BF16),
        grid=(B, G, n),
        in_specs=[big,
                  pl.BlockSpec((1, S, LANE), lambda b, g, i: (b, 0, NSA_BLK_KS + g)),
                  pl.BlockSpec((1, S, LANE), lambda b, g, i: (b, 0, NSA_BLK_VS + g)),
                  pl.BlockSpec((1, 1, t, LANE), lambda b, g, i: (b, g, i, 0)),
                  pl.BlockSpec((n, LANE, t), lambda b, g, i: (0, 0, 0))],
        out_specs=big,
        scratch_shapes=[pltpu.VMEM((NSA_HPG * t, LANE), BF16), pltpu.VMEM((NSA_HPG * t, LANE), F32),
                        pltpu.VMEM((NSA_HPG * t, LANE), F32), pltpu.VMEM((NSA_HPG * t, LANE), F32)],
        compiler_params=_cp(("parallel", "parallel", "arbitrary")),
        name="nsa_selected",
    )(z3, z3, z3, sel, _expand_blocks(S, t))


def _nsa_win_kernel(q_ref, k_ref, v_ref, oc_ref, os_ref, gl_ref, bg_ref, o_ref, q4_ref, m_ref, l_ref, acc_ref,
                    *, t, nw):
    qi = pl.program_id(2)
    _stack_heads(q_ref, q4_ref, t)
    row = lax.broadcasted_iota(jnp.int32, (t, t), 0)
    col = lax.broadcasted_iota(jnp.int32, (t, t), 1)

    def score(w, mask):
        off = pl.multiple_of((qi - w) * t, t)
        s = lax.dot_general(q4_ref[...], k_ref[0, pl.ds(off, t), :], _NT, preferred_element_type=F32)
        if mask is not None:
            s = _mask_heads(s, mask, t)
        return s, v_ref[0, pl.ds(off, t), :]

    def attend(tiles, first):
        for n, (s, v) in enumerate(tiles):
            _softmax_tile(s, v, m_ref, l_ref, acc_ref, first and n == 0)

    def older(w):
        return score(w, col > row + (w * t - NSA_WINDOW) if (w + 1) * t > NSA_WINDOW else None)

    attend([score(0, col <= row)], True)
    for w in range(1, nw):
        @pl.when(jnp.minimum(qi, nw - 1) == w)
        def _(w=w):
            attend([older(u) for u in range(1, w + 1)], False)

    o_win = _softmax_finish(l_ref, acc_ref)
    gates = jax.nn.sigmoid(gl_ref[0].astype(F32) + bg_ref[0])
    for h in range(NSA_HPG):
        cols = slice(h * LANE, (h + 1) * LANE)
        o = (gates[:, h:h + 1] * oc_ref[0, :, cols].astype(F32)
             + gates[:, 4 + h:5 + h] * os_ref[0, :, cols].astype(F32)
             + gates[:, 8 + h:9 + h] * o_win[h * t:(h + 1) * t])
        o_ref[0, :, cols] = o.astype(o_ref.dtype)


def nsa_window_merge(z3, o_cmp, o_sel, bg, *, t=256):
    B, S, _ = z3.shape
    G = NSA_KV_GROUPS
    t = min(t, S)
    n = S // t
    assert NSA_WINDOW % t == 0
    nw = min(NSA_WINDOW // t + 1, n)
    big = pl.BlockSpec((1, t, 4 * LANE), lambda b, g, i: (b, i, g))
    return pl.pallas_call(
        functools.partial(_nsa_win_kernel, t=t, nw=nw),
        out_shape=jax.ShapeDtypeStruct((B, S, NSA_HEADS * NSA_HEAD_DIM), BF16),
        grid=(B, G, n),
        in_specs=[big,
                  pl.BlockSpec((1, S, LANE), lambda b, g, i: (b, 0, NSA_BLK_KW + g)),
                  pl.BlockSpec((1, S, LANE), lambda b, g, i: (b, 0, NSA_BLK_VW + g)),
                  big, big,
                  pl.BlockSpec((1, t, LANE), lambda b, g, i: (b, i, NSA_BLK_GATE + g)),
                  pl.BlockSpec((1, 1, LANE), lambda b, g, i: (g, 0, 0))],
        out_specs=big,
        scratch_shapes=[pltpu.VMEM((NSA_HPG * t, LANE), BF16), pltpu.VMEM((NSA_HPG * t, LANE), F32),
                        pltpu.VMEM((NSA_HPG * t, LANE), F32), pltpu.VMEM((NSA_HPG * t, LANE), F32)],
        compiler_params=_cp(("parallel", "parallel", "arbitrary")),
        name="nsa_window_merge",
    )(z3, z3, z3, o_cmp, o_sel, z3, bg)


def _rot_half_cols(w):
    half = w.shape[-1] // 2
    return jnp.concatenate([-w[..., half:], w[..., :half]], axis=-1)


def _prep_ab(w_in, w_alpha_up, b_alpha, w_uq, w_ukv):
    D = w_in.shape[0]
    q_g, k_g, v_g, g_g, a_lr, c_q, c_kv, k_r = _split_cols(w_in, AB_SPLITS)
    qk = jnp.concatenate([q_g.reshape(D, GLA_HEADS, GLA_DK), k_g.reshape(D, GLA_HEADS, GLA_DK)],
                         axis=-1).reshape(D, GLA_HEADS * LANE)
    tail = jnp.concatenate([k_r, _rot_half_cols(k_r), a_lr,
                            jnp.zeros((D, 512 - 2 * MLA_ROPE - GLA_GATE_RANK), w_in.dtype)], axis=-1)
    w = jnp.concatenate([qk, v_g, g_g, c_q, c_kv, tail], axis=-1).astype(BF16)
    wa = w_alpha_up.reshape(GLA_GATE_RANK, GLA_HEADS, GLA_DK).transpose(1, 0, 2)
    wa = jnp.concatenate([wa, wa], axis=-1)
    wa = jnp.pad(wa, ((0, 0), (0, LANE - GLA_GATE_RANK), (0, 0))).astype(BF16)
    ba = b_alpha.reshape(GLA_HEADS, 1, GLA_DK)
    ba = jnp.concatenate([ba, ba], axis=-1)
    wq = w_uq.reshape(MLA_Q_RANK, MLA_HEADS, MLA_QK)
    rope = wq[..., MLA_NOPE:]
    wq = jnp.concatenate([wq, _rot_half_cols(rope)], axis=-1).transpose(1, 0, 2).astype(BF16)
    wkv = w_ukv.reshape(MLA_KV_RANK, MLA_HEADS, MLA_NOPE + MLA_V).transpose(1, 0, 2).astype(BF16)
    return w, wa, ba, wq, wkv


def _prep_nsa(w_in, b_gate):
    D = w_in.shape[0]
    q, kc, vc, ks, vs, kw, vw, gl = _split_cols(w_in, NSA_SPLITS)
    G, HG = NSA_KV_GROUPS, NSA_HPG
    glp = gl.reshape(D, G, HG, 3).transpose(0, 1, 3, 2).reshape(D, G, 3 * HG)
    glp = jnp.pad(glp, ((0, 0), (0, 0), (0, LANE - 3 * HG))).reshape(D, G * LANE)
    w = jnp.concatenate([q, kc, ks, kw, vc, vs, vw, glp], axis=-1).astype(BF16)
    bg = b_gate.reshape(G, HG, 3).transpose(0, 2, 1).reshape(G, 1, 3 * HG)
    bg = jnp.pad(bg, ((0, 0), (0, 0), (0, LANE - 3 * HG)))
    return w, bg


def _overlap_t(S):
    nr = S // NSA_CMP_STRIDE
    ns = S // NSA_SEL_LEN
    c_start = np.arange(nr) * NSA_CMP_STRIDE
    c_end = c_start + NSA_CMP_LEN
    s_start = np.arange(ns) * NSA_SEL_LEN
    s_end = s_start + NSA_SEL_LEN
    ov = (c_start[None, :] < s_end[:, None]) & (c_end[None, :] > s_start[:, None])
    return jnp.asarray(ov.astype(np.float32), dtype=BF16)


def gla_mla_mixer(h, pre_w, B, S, c2, s2, w_in, w_alpha_up, b_alpha, gla_norm_w,
                  q_norm_w, w_uq, kv_norm_w, w_ukv):
    w, wa, ba, wq, wkv = _prep_ab(w_in, w_alpha_up, b_alpha, w_uq, w_ukv)
    z = norm_matmul(h, pre_w, w)
    z3 = z.reshape(B, S, AB_Z)
    o_gla = gla(z3, wa, ba, gla_norm_w)
    qm, km, vm = mla_proj(z, q_norm_w, kv_norm_w, wq, wkv, c2, s2, B, S)
    o_mla = mla_attention(qm, km, vm)
    return [o_gla.reshape(B * S, -1), o_mla.reshape(B * S, -1)]


def nsa_mixer(h, pre_w, B, S, c128, s128, w_in, b_gate, cmp_pos, cmp_w1, cmp_w2):
    w, bg = _prep_nsa(w_in, b_gate)
    z = norm_matmul_rope(h, pre_w, w, c128, s128)
    z3 = z.reshape(B, S, NSA_Z)
    G, dh = NSA_KV_GROUPS, NSA_HEAD_DIM

    def blocks(blk):
        t = z3[:, :, blk * LANE:(blk + G) * LANE].reshape(B, S, G, dh).transpose(0, 2, 1, 3)
        return t.reshape(B, G, S // NSA_CMP_STRIDE, NSA_CMP_STRIDE * dh)

    r = jnp.stack([blocks(NSA_BLK_KC), blocks(NSA_BLK_VC)])
    kv_cmp = nsa_compress(r, cmp_pos.reshape(2, 1, NSA_CMP_LEN * dh),
                          cmp_w1.astype(BF16), cmp_w2.astype(BF16))
    o_cmp, sel = nsa_cmp_select(z3, kv_cmp, _overlap_t(S))
    o_sel = nsa_selected(z3, sel)
    o = nsa_window_merge(z3, o_cmp, o_sel, bg)
    return [o.reshape(B * S, -1)]


def kernel(x, p, positions, ln_mix_pre, ln_mix_post, ln_ffn_pre, ln_ffn_post, ab_w_in, gla_w_alpha_up, gla_b_alpha, gla_norm_w, mla_q_norm_w, mla_w_uq, mla_kv_norm_w, mla_w_ukv, ab_w_out, nsa_w_in, nsa_b_gate, nsa_cmp_pos, nsa_cmp_w1, nsa_cmp_w2, nsa_w_out, ffn_w_gate, ffn_w_up, ffn_w_down, ple_w_gate, ple_b_gate, ple_w_proj):
    B, S, D = x.shape
    T = B * S
    depth = p.shape[0]
    c2, s2, c128, s128 = rope_tables(positions)
    h = x.reshape(T, D)
    ab_out, nsa_out = ab_w_out.astype(BF16), nsa_w_out.astype(BF16)
    w_gate, w_up, w_down = ffn_w_gate.astype(BF16), ffn_w_up.astype(BF16), ffn_w_down.astype(BF16)
    ple_gate, ple_proj = ple_w_gate.astype(BF16), ple_w_proj.astype(BF16)
    p3 = p.reshape(depth, T, -1)
    for i in range(depth):
        j = i // 2
        if i % 2 == 0:
            mix = gla_mla_mixer(h, ln_mix_pre[i], B, S, c2, s2, ab_w_in[j], gla_w_alpha_up[j],
                                gla_b_alpha[j], gla_norm_w[j], mla_q_norm_w[j], mla_w_uq[j],
                                mla_kv_norm_w[j], mla_w_ukv[j])
            w_out = ab_out
        else:
            mix = nsa_mixer(h, ln_mix_pre[i], B, S, c128, s128, nsa_w_in[j], nsa_b_gate[j],
                            nsa_cmp_pos[j], nsa_cmp_w1[j], nsa_cmp_w2[j])
            w_out = nsa_out
        h = matmul_parts_norm_residual(mix, w_out, j, h, ln_mix_post[i])
        act = norm_swiglu(h, ln_ffn_pre[i], w_gate, w_up, i)
        h = matmul_norm_residual(act, w_down, i, h, ln_ffn_post[i], tk=FFN_HIDDEN // 4)
        h = ple(h, ple_gate, ple_b_gate[i], p3, ple_proj, i)
    return h.reshape(B, S, D)
```

```python
import functools

import numpy as np
import jax
import jax.numpy as jnp
from jax import lax
from jax.experimental import pallas as pl
from jax.experimental.pallas import tpu as pltpu

F32 = jnp.float32
BF16 = jnp.bfloat16

D_MODEL = 2048
PLE_DIM = 256
ROPE_THETA = 10000.0
NORM_EPS = 1e-6
NEG = -1e30
FORCE = 1e6

GLA_HEADS = 8
GLA_DK = 64
GLA_DV = 128
GLA_GATE_RANK = 16
GLA_TAU = 16.0
GLA_CHUNK = 64

MLA_HEADS = 8
MLA_Q_RANK = 512
MLA_KV_RANK = 512
MLA_NOPE = 128
MLA_ROPE = 64
MLA_V = 128
MLA_QK = MLA_NOPE + MLA_ROPE

NSA_HEADS = 16
NSA_KV_GROUPS = 4
NSA_HPG = NSA_HEADS // NSA_KV_GROUPS
NSA_HEAD_DIM = 128
NSA_CMP_LEN = 32
NSA_CMP_STRIDE = 16
NSA_SEL_LEN = 64
NSA_SEL_TOPK = 16
NSA_WINDOW = 512

FFN_HIDDEN = 5632

AB_SPLITS = (GLA_HEADS * GLA_DK, GLA_HEADS * GLA_DK, GLA_HEADS * GLA_DV, GLA_HEADS * GLA_DV,
             GLA_GATE_RANK, MLA_Q_RANK, MLA_KV_RANK, MLA_ROPE)
NSA_KV_W = NSA_KV_GROUPS * NSA_HEAD_DIM
NSA_SPLITS = (NSA_HEADS * NSA_HEAD_DIM,) + (NSA_KV_W,) * 6 + (3 * NSA_HEADS,)

LOG2E = 1.4426950408889634
LANE = 128
VMEM_LIMIT = 56 * 1024 * 1024

AB_Z = 4608
AB_BLK_QK = 0
AB_BLK_V = 8
AB_BLK_G = 16
AB_BLK_CQ = 24
AB_BLK_CKV = 28
AB_BLK_KR = 32
AB_BLK_ALR = 33

NSA_Z = 5632
NSA_BLK_Q = 0
NSA_BLK_KC = 16
NSA_BLK_KS = 20
NSA_BLK_KW = 24
NSA_BLK_VC = 28
NSA_BLK_VS = 32
NSA_BLK_VW = 36
NSA_BLK_GATE = 40
NSA_ROPE_TILES = 7
NSA_Q_TILES = 4


def _cp(sem):
    return pltpu.CompilerParams(dimension_semantics=sem, vmem_limit_bytes=VMEM_LIMIT)


def _rms(x, w):
    return x * lax.rsqrt(jnp.mean(x * x, axis=-1, keepdims=True) + NORM_EPS) * w


def _split_cols(z, widths):
    out, off = [], 0
    for w in widths:
        out.append(z[..., off:off + w])
        off += w
    return out


def _tables_kernel(pos_ref, inv_ref, c2_ref, s2_ref, c128_ref, s128_ref):
    pos = pos_ref[...].astype(F32)
    lane = lax.broadcasted_iota(jnp.int32, (1, LANE), 1)
    lo = lane < 64
    a64 = pos * inv_ref[0:1, :]
    c2_ref[...] = jnp.where(lo, jnp.cos(a64), 0.0)
    s2_ref[...] = jnp.where(lo, jnp.sin(a64), 0.0)
    a128 = pos * inv_ref[1:2, :]
    s = jnp.sin(a128)
    c128_ref[...] = jnp.cos(a128)
    s128_ref[...] = jnp.where(lo, -s, s)


def rope_tables(positions):
    T = positions.size
    inv32 = jnp.power(ROPE_THETA, -jnp.arange(0, MLA_ROPE, 2, dtype=F32) / MLA_ROPE)
    inv64 = jnp.power(ROPE_THETA, -jnp.arange(0, NSA_HEAD_DIM, 2, dtype=F32) / NSA_HEAD_DIM)
    inv = jnp.zeros((8, LANE), F32)
    inv = inv.at[0, :64].set(jnp.concatenate([inv32, inv32]))
    inv = inv.at[1, :].set(jnp.concatenate([inv64, inv64]))
    tm = min(T, 1024)
    spec = pl.BlockSpec((tm, LANE), lambda i: (i, 0))
    return pl.pallas_call(
        _tables_kernel,
        out_shape=[jax.ShapeDtypeStruct((T, LANE), F32)] * 4,
        grid=(T // tm,),
        in_specs=[pl.BlockSpec((tm, 1), lambda i: (i, 0)), pl.BlockSpec((8, LANE), lambda i: (0, 0))],
        out_specs=[spec] * 4,
        compiler_params=_cp(("parallel",)),
        name="rope_tables",
    )(positions.reshape(T, 1), inv)


def _norm_mm_kernel(x_ref, nw_ref, w_ref, o_ref, xn_ref):
    @pl.when(pl.program_id(1) == 0)
    def _():
        xn_ref[...] = _rms(x_ref[...], nw_ref[...]).astype(BF16)

    o_ref[...] = jnp.dot(xn_ref[...], w_ref[...], preferred_element_type=F32).astype(o_ref.dtype)


def _norm_mm_rope_kernel(x_ref, nw_ref, w_ref, cos_ref, sin_ref, o_ref, xn_ref, *, tn, scale):
    j = pl.program_id(1)

    @pl.when(j == 0)
    def _():
        xn_ref[...] = _rms(x_ref[...], nw_ref[...]).astype(BF16)

    mult = jnp.where(j < NSA_Q_TILES, scale, 1.0)
    is_rope = j < NSA_ROPE_TILES
    cos = jnp.where(is_rope, cos_ref[...] * mult, 1.0)
    sin = jnp.where(is_rope, sin_ref[...] * mult, 0.0)
    y = jnp.dot(xn_ref[...], w_ref[...], preferred_element_type=F32)
    for c in range(tn // LANE):
        seg = y[:, c * LANE:(c + 1) * LANE]
        o_ref[:, c * LANE:(c + 1) * LANE] = (seg * cos + pltpu.roll(seg, 64, 1) * sin).astype(o_ref.dtype)


def _norm_swiglu_kernel(x_ref, nw_ref, wg_ref, wu_ref, o_ref, xn_ref):
    @pl.when(pl.program_id(1) == 0)
    def _():
        xn_ref[...] = _rms(x_ref[...], nw_ref[...]).astype(BF16)

    xn = xn_ref[...]
    g = jnp.dot(xn, wg_ref[...], preferred_element_type=F32)
    u = jnp.dot(xn, wu_ref[...], preferred_element_type=F32)
    o_ref[...] = (g * jax.nn.sigmoid(g) * u).astype(o_ref.dtype)


def _row_tile(T, want):
    return min(T, want)


def norm_matmul(x, nw, w, *, tm=1024, tn=512):
    T, D = x.shape
    N = w.shape[1]
    tm = _row_tile(T, tm)
    return pl.pallas_call(
        _norm_mm_kernel,
        out_shape=jax.ShapeDtypeStruct((T, N), BF16),
        grid=(T // tm, N // tn),
        in_specs=[pl.BlockSpec((tm, D), lambda i, j: (i, 0)),
                  pl.BlockSpec((1, D), lambda i, j: (0, 0)),
                  pl.BlockSpec((D, tn), lambda i, j: (0, j))],
        out_specs=pl.BlockSpec((tm, tn), lambda i, j: (i, j)),
        scratch_shapes=[pltpu.VMEM((tm, D), BF16)],
        compiler_params=_cp(("parallel", "arbitrary")),
        name="norm_matmul",
    )(x, nw.reshape(1, D), w)


def norm_matmul_rope(x, nw, w, cos, sin, *, tm=1024, tn=512):
    T, D = x.shape
    N = w.shape[1]
    tm = _row_tile(T, tm)
    kern = functools.partial(_norm_mm_rope_kernel, tn=tn, scale=NSA_HEAD_DIM ** -0.5 * LOG2E)
    return pl.pallas_call(
        kern,
        out_shape=jax.ShapeDtypeStruct((T, N), BF16),
        grid=(T // tm, N // tn),
        in_specs=[pl.BlockSpec((tm, D), lambda i, j: (i, 0)),
                  pl.BlockSpec((1, D), lambda i, j: (0, 0)),
                  pl.BlockSpec((D, tn), lambda i, j: (0, j)),
                  pl.BlockSpec((tm, LANE), lambda i, j: (i, 0)),
                  pl.BlockSpec((tm, LANE), lambda i, j: (i, 0))],
        out_specs=pl.BlockSpec((tm, tn), lambda i, j: (i, j)),
        scratch_shapes=[pltpu.VMEM((tm, D), BF16)],
        compiler_params=_cp(("parallel", "arbitrary")),
        name="norm_matmul_rope",
    )(x, nw.reshape(1, D), w, cos, sin)


def norm_swiglu(x, nw, wg, wu, layer, *, tm=1024, tn=512):
    T, D = x.shape
    N = wg.shape[2]
    tm = _row_tile(T, tm)
    wspec = pl.BlockSpec((None, D, tn), lambda i, j: (layer, 0, j))
    return pl.pallas_call(
        _norm_swiglu_kernel,
        out_shape=jax.ShapeDtypeStruct((T, N), BF16),
        grid=(T // tm, N // tn),
        in_specs=[pl.BlockSpec((tm, D), lambda i, j: (i, 0)),
                  pl.BlockSpec((1, D), lambda i, j: (0, 0)),
                  wspec, wspec],
        out_specs=pl.BlockSpec((tm, tn), lambda i, j: (i, j)),
        scratch_shapes=[pltpu.VMEM((tm, D), BF16)],
        compiler_params=_cp(("parallel", "arbitrary")),
        name="norm_swiglu",
    )(x, nw.reshape(1, D), wg, wu)


def _mm_norm_res_kernel(a_ref, w_ref, h_ref, nw_ref, o_ref, acc_ref):
    k = pl.program_id(1)

    @pl.when(k == 0)
    def _():
        acc_ref[...] = jnp.zeros_like(acc_ref)

    acc_ref[...] += jnp.dot(a_ref[...], w_ref[...], preferred_element_type=F32)

    @pl.when(k == pl.num_programs(1) - 1)
    def _():
        o_ref[...] = h_ref[...] + _rms(acc_ref[...], nw_ref[...])


def _mm_norm_res_parts_kernel(*refs, widths):
    a_refs = refs[:len(widths)]
    w_ref, h_ref, nw_ref, o_ref = refs[len(widths):]
    m, off = None, 0
    for a_ref, wd in zip(a_refs, widths):
        part = jnp.dot(a_ref[...], w_ref[off:off + wd, :], preferred_element_type=F32)
        m = part if m is None else m + part
        off += wd
    o_ref[...] = h_ref[...] + _rms(m, nw_ref[...])


def matmul_parts_norm_residual(parts, w, layer, h, nw, *, tm=512):
    T = parts[0].shape[0]
    widths = tuple(a.shape[1] for a in parts)
    K, D = w.shape[1:]
    assert sum(widths) == K
    tm = _row_tile(T, tm)
    return pl.pallas_call(
        functools.partial(_mm_norm_res_parts_kernel, widths=widths),
        out_shape=jax.ShapeDtypeStruct((T, D), F32),
        grid=(T // tm,),
        in_specs=[pl.BlockSpec((tm, wd), lambda i: (i, 0)) for wd in widths]
        + [pl.BlockSpec((None, K, D), lambda i: (layer, 0, 0)),
           pl.BlockSpec((tm, D), lambda i: (i, 0)),
           pl.BlockSpec((1, D), lambda i: (0, 0))],
        out_specs=pl.BlockSpec((tm, D), lambda i: (i, 0)),
        compiler_params=_cp(("parallel",)),
        name="matmul_parts_norm_residual",
    )(*parts, w, h, nw.reshape(1, D))


def matmul_norm_residual(a, w, layer, h, nw, *, tm=512, tk=512):
    T, K = a.shape
    D = w.shape[2]
    tm = _row_tile(T, tm)
    return pl.pallas_call(
        _mm_norm_res_kernel,
        out_shape=jax.ShapeDtypeStruct((T, D), F32),
        grid=(T // tm, K // tk),
        in_specs=[pl.BlockSpec((tm, tk), lambda i, k: (i, k)),
                  pl.BlockSpec((None, tk, D), lambda i, k: (layer, k, 0)),
                  pl.BlockSpec((tm, D), lambda i, k: (i, 0)),
                  pl.BlockSpec((1, D), lambda i, k: (0, 0))],
        out_specs=pl.BlockSpec((tm, D), lambda i, k: (i, 0)),
        scratch_shapes=[pltpu.VMEM((tm, D), F32)],
        compiler_params=_cp(("parallel", "arbitrary")),
        name="matmul_norm_residual",
    )(a, w, h, nw.reshape(1, D))


def _ple_kernel(h_ref, wg_ref, bg_ref, p_ref, wp_ref, o_ref, hb_ref, *, tn):
    j = pl.program_id(1)

    @pl.when(j == 0)
    def _():
        hb_ref[...] = h_ref[...].astype(BF16)

    g = jnp.dot(hb_ref[...], wg_ref[...], preferred_element_type=F32) + bg_ref[...]
    pp = jnp.dot(p_ref[...].astype(BF16), wp_ref[...], preferred_element_type=F32)
    hs = h_ref[:, pl.ds(pl.multiple_of(j * tn, tn), tn)]
    o_ref[...] = hs + jax.nn.sigmoid(g) * pp


def ple(h, wg, bg, p, wp, layer, *, tm=1024, tn=512):
    T, D = h.shape
    P = p.shape[2]
    tm = _row_tile(T, tm)
    return pl.pallas_call(
        functools.partial(_ple_kernel, tn=tn),
        out_shape=jax.ShapeDtypeStruct((T, D), F32),
        grid=(T // tm, D // tn),
        in_specs=[pl.BlockSpec((tm, D), lambda i, j: (i, 0)),
                  pl.BlockSpec((None, D, tn), lambda i, j: (layer, 0, j)),
                  pl.BlockSpec((1, tn), lambda i, j: (0, j)),
                  pl.BlockSpec((None, tm, P), lambda i, j: (layer, i, 0)),
                  pl.BlockSpec((None, P, tn), lambda i, j: (layer, 0, j))],
        out_specs=pl.BlockSpec((tm, tn), lambda i, j: (i, j)),
        scratch_shapes=[pltpu.VMEM((tm, D), BF16)],
        compiler_params=_cp(("parallel", "arbitrary")),
        name="ple",
    )(h, wg, bg.reshape(1, D), p, wp)


def _split3(x):
    h1 = x.astype(BF16)
    r1 = x - h1.astype(F32)
    h2 = r1.astype(BF16)
    h3 = (r1 - h2.astype(F32)).astype(BF16)
    return h1, h2, h3


GLA_HEADS_PER_STEP = 2


def _gla_kernel(qk_ref, v_ref, g_ref, alr_ref, wa_ref, ba_ref, nw_ref, o_ref, st_ref, *, tr):
    L = GLA_CHUNK

    @pl.when(pl.program_id(2) == 0)
    def _():
        st_ref[...] = jnp.zeros_like(st_ref)

    row = lax.broadcasted_iota(jnp.int32, (tr, tr), 0)
    col = lax.broadcasted_iota(jnp.int32, (tr, tr), 1)
    same = lax.shift_right_logical(row, 6) == lax.shift_right_logical(col, 6)
    causal = same & (row >= col)
    tri = jnp.where(causal, 1.0, 0.0).astype(BF16)
    ones = jnp.where(same, 1.0, 0.0).astype(BF16)
    lo = lax.broadcasted_iota(jnp.int32, (1, LANE), 1) < GLA_DK
    tn = (((0,), (0,)), ((), ()))
    alr = alr_ref[0]

    for hh in range(GLA_HEADS_PER_STEP):
        lanes = slice(hh * LANE, (hh + 1) * LANE)
        x = jnp.dot(alr, wa_ref[hh], preferred_element_type=F32) + ba_ref[hh]
        log_a = (jnp.minimum(x, 0.0) - jnp.log1p(jnp.exp(-jnp.abs(x)))) * (1.0 / GLA_TAU)
        parts = _split3(log_a)
        b = functools.reduce(jnp.add, [jnp.dot(tri, a, preferred_element_type=F32) for a in parts])
        b_end = functools.reduce(jnp.add, [jnp.dot(ones, a, preferred_element_type=F32) for a in parts])
        blk = qk_ref[0, :, lanes].astype(F32)
        swp = pltpu.roll(blk, 64, 1)
        q_dec = jnp.where(lo, blk * jnp.exp(b) * (GLA_DK ** -0.5), 0.0).astype(BF16)
        k_inv = jnp.where(lo, swp * jnp.exp(-b), 0.0).astype(BF16)
        k_end = jnp.where(lo, swp * jnp.exp(b_end - b), 0.0).astype(BF16)
        decay = jnp.exp(b_end)
        v = v_ref[0, :, lanes]
        attn = lax.dot_general(q_dec, k_inv, _NT, preferred_element_type=F32)
        attn = jnp.where(causal, attn, 0.0).astype(BF16)
        o_intra = jnp.dot(attn, v, preferred_element_type=F32)

        st = st_ref[hh]
        outs = []
        for c in range(tr // L):
            rows = slice(c * L, (c + 1) * L)
            outs.append(o_intra[rows] + lax.dot_general(q_dec[rows], st.astype(BF16), _NT,
                                                        preferred_element_type=F32))
            st = st * decay[c * L:c * L + 1] + lax.dot_general(v[rows], k_end[rows], tn,
                                                               preferred_element_type=F32)
        st_ref[hh] = st
        o = jnp.concatenate(outs, axis=0)
        g = g_ref[0, :, lanes].astype(F32)
        o_ref[0, :, lanes] = (_rms(o, nw_ref[...]) * (g * jax.nn.sigmoid(g))).astype(o_ref.dtype)


def gla(z3, wa, ba, nw, *, tr=256):
    B, S, _ = z3.shape
    tr = min(tr, S)
    H = GLA_HEADS
    hp = GLA_HEADS_PER_STEP
    wide = hp * LANE
    return pl.pallas_call(
        functools.partial(_gla_kernel, tr=tr),
        out_shape=jax.ShapeDtypeStruct((B, S, H * GLA_DV), BF16),
        grid=(B, H // hp, S // tr),
        in_specs=[pl.BlockSpec((1, tr, wide), lambda b, h, r: (b, r, AB_BLK_QK // hp + h)),
                  pl.BlockSpec((1, tr, wide), lambda b, h, r: (b, r, AB_BLK_V // hp + h)),
                  pl.BlockSpec((1, tr, wide), lambda b, h, r: (b, r, AB_BLK_G // hp + h)),
                  pl.BlockSpec((1, tr, LANE), lambda b, h, r: (b, r, AB_BLK_ALR)),
                  pl.BlockSpec((hp, LANE, LANE), lambda b, h, r: (h, 0, 0)),
                  pl.BlockSpec((hp, 1, LANE), lambda b, h, r: (h, 0, 0)),
                  pl.BlockSpec((1, LANE), lambda b, h, r: (0, 0))],
        out_specs=pl.BlockSpec((1, tr, wide), lambda b, h, r: (b, r, h)),
        scratch_shapes=[pltpu.VMEM((hp, GLA_DV, LANE), F32)],
        compiler_params=_cp(("parallel", "parallel", "arbitrary")),
        name="gla",
    )(z3, z3, z3, z3, wa, ba, nw.reshape(1, GLA_DV))


def _mla_proj_kernel(cq_ref, ckv_ref, kr_ref, qnw_ref, kvnw_ref, wq_ref, wkv_ref, c2_ref, s2_ref,
                     q_ref, k_ref, v_ref, cqn_ref, ckvn_ref, kro_ref):
    @pl.when(pl.program_id(1) == 0)
    def _():
        cqn_ref[...] = _rms(cq_ref[...].astype(F32), qnw_ref[...]).astype(BF16)
        ckvn_ref[...] = _rms(ckv_ref[...].astype(F32), kvnw_ref[...]).astype(BF16)
        kr = kr_ref[...].astype(F32)
        kro_ref[...] = (kr * c2_ref[...] + pltpu.roll(kr, 64, 1) * s2_ref[...]).astype(BF16)

    scale = MLA_QK ** -0.5 * LOG2E
    yq = jnp.dot(cqn_ref[...], wq_ref[0], preferred_element_type=F32)
    y2 = yq[:, LANE:]
    qr = y2 * c2_ref[...] + pltpu.roll(y2, 64, 1) * s2_ref[...]
    q_ref[0, 0, :, :MLA_NOPE] = (yq[:, :LANE] * scale).astype(BF16)
    q_ref[0, 0, :, MLA_NOPE:] = (qr[:, :MLA_ROPE] * scale).astype(BF16)
    ykv = jnp.dot(ckvn_ref[...], wkv_ref[0], preferred_element_type=F32)
    k_ref[0, 0, :, :MLA_NOPE] = ykv[:, :LANE].astype(BF16)
    k_ref[0, 0, :, MLA_NOPE:] = kro_ref[:, :MLA_ROPE]
    v_ref[0, 0] = ykv[:, LANE:].astype(BF16)


def mla_proj(z, qnw, kvnw, wq, wkv, c2, s2, B, S, *, tm=512):
    T = z.shape[0]
    tm = min(tm, S)
    H = MLA_HEADS
    nb = S // tm

    def omap(i, h):
        return (i // nb, h, i % nb, 0)

    return pl.pallas_call(
        _mla_proj_kernel,
        out_shape=[jax.ShapeDtypeStruct((B, H, S, MLA_QK), BF16),
                   jax.ShapeDtypeStruct((B, H, S, MLA_QK), BF16),
                   jax.ShapeDtypeStruct((B, H, S, MLA_V), BF16)],
        grid=(T // tm, H),
        in_specs=[pl.BlockSpec((tm, MLA_Q_RANK), lambda i, h: (i, AB_BLK_CQ // 4)),
                  pl.BlockSpec((tm, MLA_KV_RANK), lambda i, h: (i, AB_BLK_CKV // 4)),
                  pl.BlockSpec((tm, LANE), lambda i, h: (i, AB_BLK_KR)),
                  pl.BlockSpec((1, MLA_Q_RANK), lambda i, h: (0, 0)),
                  pl.BlockSpec((1, MLA_KV_RANK), lambda i, h: (0, 0)),
                  pl.BlockSpec((1, MLA_Q_RANK, 2 * LANE), lambda i, h: (h, 0, 0)),
                  pl.BlockSpec((1, MLA_KV_RANK, 2 * LANE), lambda i, h: (h, 0, 0)),
                  pl.BlockSpec((tm, LANE), lambda i, h: (i, 0)),
                  pl.BlockSpec((tm, LANE), lambda i, h: (i, 0))],
        out_specs=[pl.BlockSpec((1, 1, tm, MLA_QK), omap),
                   pl.BlockSpec((1, 1, tm, MLA_QK), omap),
                   pl.BlockSpec((1, 1, tm, MLA_V), omap)],
        scratch_shapes=[pltpu.VMEM((tm, MLA_Q_RANK), BF16), pltpu.VMEM((tm, MLA_KV_RANK), BF16),
                        pltpu.VMEM((tm, LANE), BF16)],
        compiler_params=_cp(("parallel", "arbitrary")),
        name="mla_proj",
    )(z, z, z, qnw.reshape(1, -1), kvnw.reshape(1, -1), wq, wkv, c2, s2)


_NT = (((1,), (1,)), ((), ()))


def _softmax_tile(s, v, m_ref, l_ref, acc_ref, first):
    M, tk = s.shape
    chunks = [s[:, c * LANE:(c + 1) * LANE] for c in range(tk // LANE)]
    mrow = jnp.max(functools.reduce(jnp.maximum, chunks), axis=-1, keepdims=True)
    if first:
        m_new = jnp.broadcast_to(mrow, (M, LANE))
    else:
        m_prev = m_ref[...]
        m_new = jnp.maximum(m_prev, mrow)
    ps = [jnp.exp2(c - m_new) for c in chunks]
    lsum = functools.reduce(jnp.add, ps)
    p = (jnp.concatenate(ps, axis=1) if len(ps) > 1 else ps[0]).astype(BF16)
    pv = jnp.dot(p, v, preferred_element_type=F32)
    if first:
        l_ref[...] = lsum
        acc_ref[...] = pv
    else:
        alpha = jnp.exp2(m_prev - m_new)
        l_ref[...] = alpha * l_ref[...] + lsum
        acc_ref[...] = alpha * acc_ref[...] + pv
    m_ref[...] = m_new


def _softmax_finish(l_ref, acc_ref):
    return acc_ref[...] / jnp.sum(l_ref[...], axis=-1, keepdims=True)


def _mla_attn_kernel(q_ref, k_ref, v_ref, o_ref, m_ref, l_ref, acc_ref, *, tq, tk):
    qi = pl.program_id(2)
    r = tq // tk
    q = q_ref[0, 0]
    row = lax.broadcasted_iota(jnp.int32, (tq, tk), 0)
    col = lax.broadcasted_iota(jnp.int32, (tq, tk), 1)

    def score(j, d):
        off = pl.multiple_of(j * tk, tk)
        s = lax.dot_general(q, k_ref[0, 0, pl.ds(off, tk), :], _NT, preferred_element_type=F32)
        if d is not None:
            s = jnp.where(col + d * tk <= row, s, NEG)
        return s, v_ref[0, 0, pl.ds(off, tk), :]

    def attend(tiles, first):
        for n, (s, v) in enumerate(tiles):
            _softmax_tile(s, v, m_ref, l_ref, acc_ref, first and n == 0)

    attend([score(qi * r + d, d) for d in range(r)], True)

    def body(i, carry):
        attend([score(2 * i, None), score(2 * i + 1, None)], False)
        return carry

    lax.fori_loop(0, qi * (r // 2), body, 0)
    o_ref[0] = _softmax_finish(l_ref, acc_ref).astype(o_ref.dtype)


def mla_attention(q, k, v, *, tq=512, tk=256):
    B, H, S, _ = q.shape
    tq = min(tq, S)
    tk = min(tk, tq // 2)
    assert tq % (2 * tk) == 0
    return pl.pallas_call(
        functools.partial(_mla_attn_kernel, tq=tq, tk=tk),
        out_shape=jax.ShapeDtypeStruct((B, S, H * MLA_V), BF16),
        grid=(B, H, S // tq),
        in_specs=[pl.BlockSpec((1, 1, tq, MLA_QK), lambda b, h, i: (b, h, i, 0)),
                  pl.BlockSpec((1, 1, S, MLA_QK), lambda b, h, i: (b, h, 0, 0)),
                  pl.BlockSpec((1, 1, S, MLA_V), lambda b, h, i: (b, h, 0, 0))],
        out_specs=pl.BlockSpec((1, tq, MLA_V), lambda b, h, i: (b, i, h)),
        scratch_shapes=[pltpu.VMEM((tq, LANE), F32), pltpu.VMEM((tq, LANE), F32),
                        pltpu.VMEM((tq, MLA_V), F32)],
        compiler_params=_cp(("parallel", "parallel", "arbitrary")),
        name="mla_attention",
    )(q, k, v)


def _nsa_compress_kernel(r_ref, pos_ref, w1_ref, w2_ref, o_ref):
    r = r_ref[0, 0, 0]
    half = r.shape[1]
    nr = r.shape[0]
    a = jnp.dot(r, w1_ref[0, :half, :], preferred_element_type=F32)
    b = jnp.dot(r, w1_ref[0, half:, :], preferred_element_type=F32)
    pos = jnp.broadcast_to(pos_ref[0], (8, 2 * half)).astype(BF16)
    c = jnp.dot(pos, w1_ref[0], preferred_element_type=F32)[0:1, :]
    pre = a + pltpu.roll(b, nr - 1, 0) + c
    o_ref[0, 0, 0] = jnp.dot(jax.nn.gelu(pre).astype(BF16), w2_ref[0],
                             preferred_element_type=F32).astype(o_ref.dtype)


def nsa_compress(r, pos, w1, w2):
    _, B, G, NR, W = r.shape
    dh = NSA_HEAD_DIM
    return pl.pallas_call(
        _nsa_compress_kernel,
        out_shape=jax.ShapeDtypeStruct((2, B, G, NR, dh), BF16),
        grid=(2, B, G),
        in_specs=[pl.BlockSpec((1, 1, 1, NR, W), lambda c, b, g: (c, b, g, 0, 0)),
                  pl.BlockSpec((1, 1, 2 * W), lambda c, b, g: (c, 0, 0)),
                  pl.BlockSpec((1, 2 * W, dh), lambda c, b, g: (c, 0, 0)),
                  pl.BlockSpec((1, dh, dh), lambda c, b, g: (c, 0, 0))],
        out_specs=pl.BlockSpec((1, 1, 1, NR, dh), lambda c, b, g: (c, b, g, 0, 0)),
        compiler_params=_cp(("parallel", "parallel", "parallel")),
        name="nsa_compress",
    )(r, pos, w1, w2)


def _nsa_cmp_kernel(q_ref, kc_ref, vc_ref, ovt_ref, o_ref, sel_ref, *, tq, ns, ksel):
    qi = pl.program_id(2)
    nc = kc_ref.shape[3]
    kc = kc_ref[0, 0, 0]
    vc = vc_ref[0, 0, 0]
    t = qi * tq + lax.broadcasted_iota(jnp.int32, (tq, 1), 0)
    n = lax.broadcasted_iota(jnp.int32, (1, nc), 1)
    ok = (n * NSA_CMP_STRIDE + (NSA_CMP_LEN - 1)) <= t
    nt = (((1,), (1,)), ((), ()))
    psum = jnp.zeros((tq, nc), F32)
    for j in range(NSA_HPG):
        q = q_ref[0, :, j * LANE:(j + 1) * LANE]
        s = lax.dot_general(q, kc, nt, preferred_element_type=F32)
        s = jnp.where(ok, s, NEG)
        e = jnp.where(ok, jnp.exp2(s - jnp.max(s, axis=-1, keepdims=True)), 0.0)
        d = jnp.sum(e, axis=-1, keepdims=True)
        p = e * jnp.where(d > 0.0, 1.0 / d, 0.0)
        o_ref[0, :, j * LANE:(j + 1) * LANE] = jnp.dot(p.astype(BF16), vc,
                                                        preferred_element_type=F32).astype(o_ref.dtype)
        psum = psum + p
    ph = psum.astype(BF16)
    plo = (psum - ph.astype(F32)).astype(BF16)
    ovt = ovt_ref[...]
    imp = (lax.dot_general(ovt, ph, nt, preferred_element_type=F32)
           + lax.dot_general(ovt, plo, nt, preferred_element_type=F32))
    m = lax.broadcasted_iota(jnp.int32, (ns, 1), 0)
    tt = qi * tq + lax.broadcasted_iota(jnp.int32, (1, tq), 1)
    causal = m * NSA_SEL_LEN <= tt
    cur = lax.shift_right_logical(tt, 6)
    forced = (m == 0) | (m == cur) | (m == cur - 1)
    score = jnp.where(causal, jnp.where(forced, FORCE, imp), -FORCE)
    cnt = jnp.zeros((ns, tq), F32)
    for m2 in range(ns):
        r = score[m2:m2 + 1, :]
        ahead = (r > score) | ((r == score) & (m2 < m))
        cnt = cnt + jnp.where(ahead, 1.0, 0.0)
    sel = jnp.where((cnt < float(ksel)) & causal, 1.0, 0.0)
    selp = jnp.concatenate([sel, jnp.zeros((LANE - ns, tq), F32)], axis=0)
    sel_ref[0, 0] = selp.T.astype(sel_ref.dtype)


def nsa_cmp_select(z3, kv_cmp, ovt, *, tq=512):
    B, S, _ = z3.shape
    G = NSA_KV_GROUPS
    tq = min(tq, S)
    ns = S // NSA_SEL_LEN
    nc = kv_cmp.shape[3]
    kern = functools.partial(_nsa_cmp_kernel, tq=tq, ns=ns, ksel=min(NSA_SEL_TOPK, ns))
    return pl.pallas_call(
        kern,
        out_shape=[jax.ShapeDtypeStruct((B, S, NSA_HEADS * NSA_HEAD_DIM), BF16),
                   jax.ShapeDtypeStruct((B, G, S, LANE), BF16)],
        grid=(B, G, S // tq),
        in_specs=[pl.BlockSpec((1, tq, 4 * LANE), lambda b, g, i: (b, i, g)),
                  pl.BlockSpec((1, 1, 1, nc, LANE), lambda b, g, i: (0, b, g, 0, 0)),
                  pl.BlockSpec((1, 1, 1, nc, LANE), lambda b, g, i: (1, b, g, 0, 0)),
                  pl.BlockSpec((ns, nc), lambda b, g, i: (0, 0))],
        out_specs=[pl.BlockSpec((1, tq, 4 * LANE), lambda b, g, i: (b, i, g)),
                   pl.BlockSpec((1, 1, tq, LANE), lambda b, g, i: (b, g, i, 0))],
        compiler_params=_cp(("parallel", "parallel", "parallel")),
        name="nsa_cmp_select",
    )(z3, kv_cmp, kv_cmp, ovt)


def _stack_heads(q_ref, q4_ref, t):
    for h in range(NSA_HPG):
        q4_ref[h * t:(h + 1) * t, :] = q_ref[0, :, h * LANE:(h + 1) * LANE]


def _mask_heads(s, mask, t):
    return jnp.concatenate([jnp.where(mask, s[h * t:(h + 1) * t], NEG) for h in range(NSA_HPG)], axis=0)


def _nsa_sel_kernel(q_ref, k_ref, v_ref, sel_ref, e_ref, o_ref, q4_ref, m_ref, l_ref, acc_ref, *, t):
    qi = pl.program_id(2)
    _stack_heads(q_ref, q4_ref, t)
    sel = sel_ref[0, 0]
    row = lax.broadcasted_iota(jnp.int32, (t, t), 0)
    col = lax.broadcasted_iota(jnp.int32, (t, t), 1)

    def score(j, diag):
        off = pl.multiple_of(j * t, t)
        mask = jnp.dot(sel, e_ref[j], preferred_element_type=F32) > 0.5
        if diag:
            mask = mask & (col <= row)
        s = lax.dot_general(q4_ref[...], k_ref[0, pl.ds(off, t), :], _NT, preferred_element_type=F32)
        return _mask_heads(s, mask, t), v_ref[0, pl.ds(off, t), :]

    def attend(tiles, first):
        for n, (s, v) in enumerate(tiles):
            _softmax_tile(s, v, m_ref, l_ref, acc_ref, first and n == 0)

    attend([score(qi, True)], True)
    odd = qi & 1

    @pl.when(odd == 1)
    def _():
        attend([score(0, False)], False)

    def body(i, carry):
        j = odd + 2 * i
        attend([score(j, False), score(j + 1, False)], False)
        return carry

    lax.fori_loop(0, lax.shift_right_logical(qi, 1), body, 0)
    o = _softmax_finish(l_ref, acc_ref)
    for h in range(NSA_HPG):
        o_ref[0, :, h * LANE:(h + 1) * LANE] = o[h * t:(h + 1) * t].astype(o_ref.dtype)


def _expand_blocks(S, t):
    key = np.arange(S).reshape(S // t, 1, t)
    blk = np.arange(LANE).reshape(1, LANE, 1)
    return jnp.asarray((key // NSA_SEL_LEN == blk).astype(np.float32), dtype=BF16)


def nsa_selected(z3, sel, *, t=256):
    B, S, _ = z3.shape
    G = NSA_KV_GROUPS
    t = min(t, S)
    n = S // t
    big = pl.BlockSpec((1, t, 4 * LANE), lambda b, g, i: (b, i, g))
    return pl.pallas_call(
        functools.partial(_nsa_sel_kernel, t=t),
        out_shape=jax.ShapeDtypeStruct((B, S, NSA_HEADS * NSA_HEAD_DIM), BF16),
        grid=(B, G, n),
        in_specs=[big,
                  pl.BlockSpec((1, S, LANE), lambda b, g, i: (b, 0, NSA_BLK_KS + g)),
                  pl.BlockSpec((1, S, LANE), lambda b, g, i: (b, 0, NSA_BLK_VS + g)),
                  pl.BlockSpec((1, 1, t, LANE), lambda b, g, i: (b, g, i, 0)),
                  pl.BlockSpec((n, LANE, t), lambda b, g, i: (0, 0, 0))],
        out_specs=big,
        scratch_shapes=[pltpu.VMEM((NSA_HPG * t, LANE), BF16), pltpu.VMEM((NSA_HPG * t, LANE), F32),
                        pltpu.VMEM((NSA_HPG * t, LANE), F32), pltpu.VMEM((NSA_HPG * t, LANE), F32)],
        compiler_params=_cp(("parallel", "parallel", "arbitrary")),
        name="nsa_selected",
    )(z3, z3, z3, sel, _expand_blocks(S, t))


def _nsa_win_kernel(q_ref, k_ref, v_ref, oc_ref, os_ref, gl_ref, bg_ref, o_ref, q4_ref, m_ref, l_ref, acc_ref,
                    *, t, nw):
    qi = pl.program_id(2)
    _stack_heads(q_ref, q4_ref, t)
    row = lax.broadcasted_iota(jnp.int32, (t, t), 0)
    col = lax.broadcasted_iota(jnp.int32, (t, t), 1)

    def score(w, mask):
        off = pl.multiple_of((qi - w) * t, t)
        s = lax.dot_general(q4_ref[...], k_ref[0, pl.ds(off, t), :], _NT, preferred_element_type=F32)
        if mask is not None:
            s = _mask_heads(s, mask, t)
        return s, v_ref[0, pl.ds(off, t), :]

    def attend(tiles, first):
        for n, (s, v) in enumerate(tiles):
            _softmax_tile(s, v, m_ref, l_ref, acc_ref, first and n == 0)

    def older(w):
        return score(w, col > row + (w * t - NSA_WINDOW) if (w + 1) * t > NSA_WINDOW else None)

    attend([score(0, col <= row)], True)
    for w in range(1, nw):
        @pl.when(jnp.minimum(qi, nw - 1) == w)
        def _(w=w):
            attend([older(u) for u in range(1, w + 1)], False)

    o_win = _softmax_finish(l_ref, acc_ref)
    gates = jax.nn.sigmoid(gl_ref[0].astype(F32) + bg_ref[0])
    for h in range(NSA_HPG):
        cols = slice(h * LANE, (h + 1) * LANE)
        o = (gates[:, h:h + 1] * oc_ref[0, :, cols].astype(F32)
             + gates[:, 4 + h:5 + h] * os_ref[0, :, cols].astype(F32)
             + gates[:, 8 + h:9 + h] * o_win[h * t:(h + 1) * t])
        o_ref[0, :, cols] = o.astype(o_ref.dtype)


def nsa_window_merge(z3, o_cmp, o_sel, bg, *, t=256):
    B, S, _ = z3.shape
    G = NSA_KV_GROUPS
    t = min(t, S)
    n = S // t
    assert NSA_WINDOW % t == 0
    nw = min(NSA_WINDOW // t + 1, n)
    big = pl.BlockSpec((1, t, 4 * LANE), lambda b, g, i: (b, i, g))
    return pl.pallas_call(
        functools.partial(_nsa_win_kernel, t=t, nw=nw),
        out_shape=jax.ShapeDtypeStruct((B, S, NSA_HEADS * NSA_HEAD_DIM), BF16),
        grid=(B, G, n),
        in_specs=[big,
                  pl.BlockSpec((1, S, LANE), lambda b, g, i: (b, 0, NSA_BLK_KW + g)),
                  pl.BlockSpec((1, S, LANE), lambda b, g, i: (b, 0, NSA_BLK_VW + g)),
                  big, big,
                  pl.BlockSpec((1, t, LANE), lambda b, g, i: (b, i, NSA_BLK_GATE + g)),
                  pl.BlockSpec((1, 1, LANE), lambda b, g, i: (g, 0, 0))],
        out_specs=big,
        scratch_shapes=[pltpu.VMEM((NSA_HPG * t, LANE), BF16), pltpu.VMEM((NSA_HPG * t, LANE), F32),
                        pltpu.VMEM((NSA_HPG * t, LANE), F32), pltpu.VMEM((NSA_HPG * t, LANE), F32)],
        compiler_params=_cp(("parallel", "parallel", "arbitrary")),
        name="nsa_window_merge",
    )(z3, z3, z3, o_cmp, o_sel, z3, bg)


def _rot_half_cols(w):
    half = w.shape[-1] // 2
    return jnp.concatenate([-w[..., half:], w[..., :half]], axis=-1)


def _prep_ab(w_in, w_alpha_up, b_alpha, w_uq, w_ukv):
    D = w_in.shape[0]
    q_g, k_g, v_g, g_g, a_lr, c_q, c_kv, k_r = _split_cols(w_in, AB_SPLITS)
    qk = jnp.concatenate([q_g.reshape(D, GLA_HEADS, GLA_DK), k_g.reshape(D, GLA_HEADS, GLA_DK)],
                         axis=-1).reshape(D, GLA_HEADS * LANE)
    tail = jnp.concatenate([k_r, _rot_half_cols(k_r), a_lr,
                            jnp.zeros((D, 512 - 2 * MLA_ROPE - GLA_GATE_RANK), w_in.dtype)], axis=-1)
    w = jnp.concatenate([qk, v_g, g_g, c_q, c_kv, tail], axis=-1).astype(BF16)
    wa = w_alpha_up.reshape(GLA_GATE_RANK, GLA_HEADS, GLA_DK).transpose(1, 0, 2)
    wa = jnp.concatenate([wa, wa], axis=-1)
    wa = jnp.pad(wa, ((0, 0), (0, LANE - GLA_GATE_RANK), (0, 0))).astype(BF16)
    ba = b_alpha.reshape(GLA_HEADS, 1, GLA_DK)
    ba = jnp.concatenate([ba, ba], axis=-1)
    wq = w_uq.reshape(MLA_Q_RANK, MLA_HEADS, MLA_QK)
    rope = wq[..., MLA_NOPE:]
    wq = jnp.concatenate([wq, _rot_half_cols(rope)], axis=-1).transpose(1, 0, 2).astype(BF16)
    wkv = w_ukv.reshape(MLA_KV_RANK, MLA_HEADS, MLA_NOPE + MLA_V).transpose(1, 0, 2).astype(BF16)
    return w, wa, ba, wq, wkv


def _prep_nsa(w_in, b_gate):
    D = w_in.shape[0]
    q, kc, vc, ks, vs, kw, vw, gl = _split_cols(w_in, NSA_SPLITS)
    G, HG = NSA_KV_GROUPS, NSA_HPG
    glp = gl.reshape(D, G, HG, 3).transpose(0, 1, 3, 2).reshape(D, G, 3 * HG)
    glp = jnp.pad(glp, ((0, 0), (0, 0), (0, LANE - 3 * HG))).reshape(D, G * LANE)
    w = jnp.concatenate([q, kc, ks, kw, vc, vs, vw, glp], axis=-1).astype(BF16)
    bg = b_gate.reshape(G, HG, 3).transpose(0, 2, 1).reshape(G, 1, 3 * HG)
    bg = jnp.pad(bg, ((0, 0), (0, 0), (0, LANE - 3 * HG)))
    return w, bg


def _overlap_t(S):
    nr = S // NSA_CMP_STRIDE
    ns = S // NSA_SEL_LEN
    c_start = np.arange(nr) * NSA_CMP_STRIDE
    c_end = c_start + NSA_CMP_LEN
    s_start = np.arange(ns) * NSA_SEL_LEN
    s_end = s_start + NSA_SEL_LEN
    ov = (c_start[None, :] < s_end[:, None]) & (c_end[None, :] > s_start[:, None])
    return jnp.asarray(ov.astype(np.float32), dtype=BF16)


def gla_mla_mixer(h, pre_w, B, S, c2, s2, w_in, w_alpha_up, b_alpha, gla_norm_w,
                  q_norm_w, w_uq, kv_norm_w, w_ukv):
    w, wa, ba, wq, wkv = _prep_ab(w_in, w_alpha_up, b_alpha, w_uq, w_ukv)
    z = norm_matmul(h, pre_w, w)
    z3 = z.reshape(B, S, AB_Z)
    o_gla = gla(z3, wa, ba, gla_norm_w)
    qm, km, vm = mla_proj(z, q_norm_w, kv_norm_w, wq, wkv, c2, s2, B, S)
    o_mla = mla_attention(qm, km, vm)
    return [o_gla.reshape(B * S, -1), o_mla.reshape(B * S, -1)]


def nsa_mixer(h, pre_w, B, S, c128, s128, w_in, b_gate, cmp_pos, cmp_w1, cmp_w2):
    w, bg = _prep_nsa(w_in, b_gate)
    z = norm_matmul_rope(h, pre_w, w, c128, s128)
    z3 = z.reshape(B, S, NSA_Z)
    G, dh = NSA_KV_GROUPS, NSA_HEAD_DIM

    def blocks(blk):
        t = z3[:, :, blk * LANE:(blk + G) * LANE].reshape(B, S, G, dh).transpose(0, 2, 1, 3)
        return t.reshape(B, G, S // NSA_CMP_STRIDE, NSA_CMP_STRIDE * dh)

    r = jnp.stack([blocks(NSA_BLK_KC), blocks(NSA_BLK_VC)])
    kv_cmp = nsa_compress(r, cmp_pos.reshape(2, 1, NSA_CMP_LEN * dh),
                          cmp_w1.astype(BF16), cmp_w2.astype(BF16))
    o_cmp, sel = nsa_cmp_select(z3, kv_cmp, _overlap_t(S))
    o_sel = nsa_selected(z3, sel)
    o = nsa_window_merge(z3, o_cmp, o_sel, bg)
    return [o.reshape(B * S, -1)]


def kernel(x, p, positions, ln_mix_pre, ln_mix_post, ln_ffn_pre, ln_ffn_post, ab_w_in, gla_w_alpha_up, gla_b_alpha, gla_norm_w, mla_q_norm_w, mla_w_uq, mla_kv_norm_w, mla_w_ukv, ab_w_out, nsa_w_in, nsa_b_gate, nsa_cmp_pos, nsa_cmp_w1, nsa_cmp_w2, nsa_w_out, ffn_w_gate, ffn_w_up, ffn_w_down, ple_w_gate, ple_b_gate, ple_w_proj):
    B, S, D = x.shape
    T = B * S
    depth = p.shape[0]
    c2, s2, c128, s128 = rope_tables(positions)
    h = x.reshape(T, D)
    ab_out, nsa_out = ab_w_out.astype(BF16), nsa_w_out.astype(BF16)
    w_gate, w_up, w_down = ffn_w_gate.astype(BF16), ffn_w_up.astype(BF16), ffn_w_down.astype(BF16)
    ple_gate, ple_proj = ple_w_gate.astype(BF16), ple_w_proj.astype(BF16)
    p3 = p.reshape(depth, T, -1)
    for i in range(depth):
        j = i // 2
        if i % 2 == 0:
            mix = gla_mla_mixer(h, ln_mix_pre[i], B, S, c2, s2, ab_w_in[j], gla_w_alpha_up[j],
                                gla_b_alpha[j], gla_norm_w[j], mla_q_norm_w[j], mla_w_uq[j],
                                mla_kv_norm_w[j], mla_w_ukv[j])
            w_out = ab_out
        else:
            mix = nsa_mixer(h, ln_mix_pre[i], B, S, c128, s128, nsa_w_in[j], nsa_b_gate[j],
                            nsa_cmp_pos[j], nsa_cmp_w1[j], nsa_cmp_w2[j])
            w_out = nsa_out
        h = matmul_parts_norm_residual(mix, w_out, j, h, ln_mix_post[i])
        act = norm_swiglu(h, ln_ffn_pre[i], w_gate, w_up, i)
        h = matmul_norm_residual(act, w_down, i, h, ln_ffn_post[i], tm=1024, tk=512)
        h = ple(h, ple_gate, ple_b_gate[i], p3, ple_proj, i)
    return h.reshape(B, S, D)
```

```python
import functools

import numpy as np
import jax
import jax.numpy as jnp
from jax import lax
from jax.experimental import pallas as pl
from jax.experimental.pallas import tpu as pltpu

F32 = jnp.float32
BF16 = jnp.bfloat16

D_MODEL = 2048
PLE_DIM = 256
ROPE_THETA = 10000.0
NORM_EPS = 1e-6
NEG = -1e30
FORCE = 1e6

GLA_HEADS = 8
GLA_DK = 64
GLA_DV = 128
GLA_GATE_RANK = 16
GLA_TAU = 16.0
GLA_CHUNK = 64

MLA_HEADS = 8
MLA_Q_RANK = 512
MLA_KV_RANK = 512
MLA_NOPE = 128
MLA_ROPE = 64
MLA_V = 128
MLA_QK = MLA_NOPE + MLA_ROPE

NSA_HEADS = 16
NSA_KV_GROUPS = 4
NSA_HPG = NSA_HEADS // NSA_KV_GROUPS
NSA_HEAD_DIM = 128
NSA_CMP_LEN = 32
NSA_CMP_STRIDE = 16
NSA_SEL_LEN = 64
NSA_SEL_TOPK = 16
NSA_WINDOW = 512

FFN_HIDDEN = 5632

AB_SPLITS = (GLA_HEADS * GLA_DK, GLA_HEADS * GLA_DK, GLA_HEADS * GLA_DV, GLA_HEADS * GLA_DV,
             GLA_GATE_RANK, MLA_Q_RANK, MLA_KV_RANK, MLA_ROPE)
NSA_KV_W = NSA_KV_GROUPS * NSA_HEAD_DIM
NSA_SPLITS = (NSA_HEADS * NSA_HEAD_DIM,) + (NSA_KV_W,) * 6 + (3 * NSA_HEADS,)

LOG2E = 1.4426950408889634
LANE = 128
VMEM_LIMIT = 56 * 1024 * 1024

AB_Z = 4608
AB_BLK_QK = 0
AB_BLK_V = 8
AB_BLK_G = 16
AB_BLK_CQ = 24
AB_BLK_CKV = 28
AB_BLK_KR = 32
AB_BLK_ALR = 33

NSA_Z = 5632
NSA_BLK_Q = 0
NSA_BLK_KC = 16
NSA_BLK_KS = 20
NSA_BLK_KW = 24
NSA_BLK_VC = 28
NSA_BLK_VS = 32
NSA_BLK_VW = 36
NSA_BLK_GATE = 40
NSA_ROPE_TILES = 7
NSA_Q_TILES = 4


def _cp(sem):
    return pltpu.CompilerParams(dimension_semantics=sem, vmem_limit_bytes=VMEM_LIMIT)


def _rms(x, w):
    return x * lax.rsqrt(jnp.mean(x * x, axis=-1, keepdims=True) + NORM_EPS) * w


def _split_cols(z, widths):
    out, off = [], 0
    for w in widths:
        out.append(z[..., off:off + w])
        off += w
    return out


def _tables_kernel(pos_ref, inv_ref, c2_ref, s2_ref, c128_ref, s128_ref):
    pos = pos_ref[...].astype(F32)
    lane = lax.broadcasted_iota(jnp.int32, (1, LANE), 1)
    lo = lane < 64
    a64 = pos * inv_ref[0:1, :]
    c2_ref[...] = jnp.where(lo, jnp.cos(a64), 0.0)
    s2_ref[...] = jnp.where(lo, jnp.sin(a64), 0.0)
    a128 = pos * inv_ref[1:2, :]
    s = jnp.sin(a128)
    c128_ref[...] = jnp.cos(a128)
    s128_ref[...] = jnp.where(lo, -s, s)


def rope_tables(positions):
    T = positions.size
    inv32 = jnp.power(ROPE_THETA, -jnp.arange(0, MLA_ROPE, 2, dtype=F32) / MLA_ROPE)
    inv64 = jnp.power(ROPE_THETA, -jnp.arange(0, NSA_HEAD_DIM, 2, dtype=F32) / NSA_HEAD_DIM)
    inv = jnp.zeros((8, LANE), F32)
    inv = inv.at[0, :64].set(jnp.concatenate([inv32, inv32]))
    inv = inv.at[1, :].set(jnp.concatenate([inv64, inv64]))
    tm = min(T, 1024)
    spec = pl.BlockSpec((tm, LANE), lambda i: (i, 0))
    return pl.pallas_call(
        _tables_kernel,
        out_shape=[jax.ShapeDtypeStruct((T, LANE), F32)] * 4,
        grid=(T // tm,),
        in_specs=[pl.BlockSpec((tm, 1), lambda i: (i, 0)), pl.BlockSpec((8, LANE), lambda i: (0, 0))],
        out_specs=[spec] * 4,
        compiler_params=_cp(("parallel",)),
        name="rope_tables",
    )(positions.reshape(T, 1), inv)


def _norm_mm_kernel(x_ref, nw_ref, w_ref, o_ref, xn_ref):
    @pl.when(pl.program_id(1) == 0)
    def _():
        xn_ref[...] = _rms(x_ref[...], nw_ref[...]).astype(BF16)

    o_ref[...] = jnp.dot(xn_ref[...], w_ref[...], preferred_element_type=F32).astype(o_ref.dtype)


def _norm_mm_rope_kernel(x_ref, nw_ref, w_ref, cos_ref, sin_ref, o_ref, xn_ref, *, tn, scale):
    j = pl.program_id(1)

    @pl.when(j == 0)
    def _():
        xn_ref[...] = _rms(x_ref[...], nw_ref[...]).astype(BF16)

    mult = jnp.where(j < NSA_Q_TILES, scale, 1.0)
    is_rope = j < NSA_ROPE_TILES
    cos = jnp.where(is_rope, cos_ref[...] * mult, 1.0)
    sin = jnp.where(is_rope, sin_ref[...] * mult, 0.0)
    y = jnp.dot(xn_ref[...], w_ref[...], preferred_element_type=F32)
    for c in range(tn // LANE):
        seg = y[:, c * LANE:(c + 1) * LANE]
        o_ref[:, c * LANE:(c + 1) * LANE] = (seg * cos + pltpu.roll(seg, 64, 1) * sin).astype(o_ref.dtype)


def _norm_swiglu_kernel(x_ref, nw_ref, wg_ref, wu_ref, o_ref, xn_ref):
    @pl.when(pl.program_id(1) == 0)
    def _():
        xn_ref[...] = _rms(x_ref[...], nw_ref[...]).astype(BF16)

    xn = xn_ref[...]
    g = jnp.dot(xn, wg_ref[...], preferred_element_type=F32)
    u = jnp.dot(xn, wu_ref[...], preferred_element_type=F32)
    o_ref[...] = (g * jax.nn.sigmoid(g) * u).astype(o_ref.dtype)


def _row_tile(T, want):
    return min(T, want)


def norm_matmul(x, nw, w, *, tm=1024, tn=512):
    T, D = x.shape
    N = w.shape[1]
    tm = _row_tile(T, tm)
    return pl.pallas_call(
        _norm_mm_kernel,
        out_shape=jax.ShapeDtypeStruct((T, N), BF16),
        grid=(T // tm, N // tn),
        in_specs=[pl.BlockSpec((tm, D), lambda i, j: (i, 0)),
                  pl.BlockSpec((1, D), lambda i, j: (0, 0)),
                  pl.BlockSpec((D, tn), lambda i, j: (0, j))],
        out_specs=pl.BlockSpec((tm, tn), lambda i, j: (i, j)),
        scratch_shapes=[pltpu.VMEM((tm, D), BF16)],
        compiler_params=_cp(("parallel", "arbitrary")),
        name="norm_matmul",
    )(x, nw.reshape(1, D), w)


def norm_matmul_rope(x, nw, w, cos, sin, *, tm=1024, tn=512):
    T, D = x.shape
    N = w.shape[1]
    tm = _row_tile(T, tm)
    kern = functools.partial(_norm_mm_rope_kernel, tn=tn, scale=NSA_HEAD_DIM ** -0.5 * LOG2E)
    return pl.pallas_call(
        kern,
        out_shape=jax.ShapeDtypeStruct((T, N), BF16),
        grid=(T // tm, N // tn),
        in_specs=[pl.BlockSpec((tm, D), lambda i, j: (i, 0)),
                  pl.BlockSpec((1, D), lambda i, j: (0, 0)),
                  pl.BlockSpec((D, tn), lambda i, j: (0, j)),
                  pl.BlockSpec((tm, LANE), lambda i, j: (i, 0)),
                  pl.BlockSpec((tm, LANE), lambda i, j: (i, 0))],
        out_specs=pl.BlockSpec((tm, tn), lambda i, j: (i, j)),
        scratch_shapes=[pltpu.VMEM((tm, D), BF16)],
        compiler_params=_cp(("parallel", "arbitrary")),
        name="norm_matmul_rope",
    )(x, nw.reshape(1, D), w, cos, sin)


def norm_swiglu(x, nw, wg, wu, layer, *, tm=1024, tn=512):
    T, D = x.shape
    N = wg.shape[2]
    tm = _row_tile(T, tm)
    wspec = pl.BlockSpec((None, D, tn), lambda i, j: (layer, 0, j))
    return pl.pallas_call(
        _norm_swiglu_kernel,
        out_shape=jax.ShapeDtypeStruct((T, N), BF16),
        grid=(T // tm, N // tn),
        in_specs=[pl.BlockSpec((tm, D), lambda i, j: (i, 0)),
                  pl.BlockSpec((1, D), lambda i, j: (0, 0)),
                  wspec, wspec],
        out_specs=pl.BlockSpec((tm, tn), lambda i, j: (i, j)),
        scratch_shapes=[pltpu.VMEM((tm, D), BF16)],
        compiler_params=_cp(("parallel", "arbitrary")),
        name="norm_swiglu",
    )(x, nw.reshape(1, D), wg, wu)


def _mm_norm_res_kernel(a_ref, w_ref, h_ref, nw_ref, o_ref, acc_ref):
    k = pl.program_id(1)

    @pl.when(k == 0)
    def _():
        acc_ref[...] = jnp.zeros_like(acc_ref)

    acc_ref[...] += jnp.dot(a_ref[...], w_ref[...], preferred_element_type=F32)

    @pl.when(k == pl.num_programs(1) - 1)
    def _():
        o_ref[...] = h_ref[...] + _rms(acc_ref[...], nw_ref[...])


def _mm_norm_res_parts_kernel(*refs, widths):
    a_refs = refs[:len(widths)]
    w_ref, h_ref, nw_ref, o_ref = refs[len(widths):]
    m, off = None, 0
    for a_ref, wd in zip(a_refs, widths):
        part = jnp.dot(a_ref[...], w_ref[off:off + wd, :], preferred_element_type=F32)
        m = part if m is None else m + part
        off += wd
    o_ref[...] = h_ref[...] + _rms(m, nw_ref[...])


def matmul_parts_norm_residual(parts, w, layer, h, nw, *, tm=512):
    T = parts[0].shape[0]
    widths = tuple(a.shape[1] for a in parts)
    K, D = w.shape[1:]
    assert sum(widths) == K
    tm = _row_tile(T, tm)
    return pl.pallas_call(
        functools.partial(_mm_norm_res_parts_kernel, widths=widths),
        out_shape=jax.ShapeDtypeStruct((T, D), F32),
        grid=(T // tm,),
        in_specs=[pl.BlockSpec((tm, wd), lambda i: (i, 0)) for wd in widths]
        + [pl.BlockSpec((None, K, D), lambda i: (layer, 0, 0)),
           pl.BlockSpec((tm, D), lambda i: (i, 0)),
           pl.BlockSpec((1, D), lambda i: (0, 0))],
        out_specs=pl.BlockSpec((tm, D), lambda i: (i, 0)),
        compiler_params=_cp(("parallel",)),
        name="matmul_parts_norm_residual",
    )(*parts, w, h, nw.reshape(1, D))


def matmul_norm_residual(a, w, layer, h, nw, *, tm=512, tk=512):
    T, K = a.shape
    D = w.shape[2]
    tm = _row_tile(T, tm)
    return pl.pallas_call(
        _mm_norm_res_kernel,
        out_shape=jax.ShapeDtypeStruct((T, D), F32),
        grid=(T // tm, K // tk),
        in_specs=[pl.BlockSpec((tm, tk), lambda i, k: (i, k)),
                  pl.BlockSpec((None, tk, D), lambda i, k: (layer, k, 0)),
                  pl.BlockSpec((tm, D), lambda i, k: (i, 0)),
                  pl.BlockSpec((1, D), lambda i, k: (0, 0))],
        out_specs=pl.BlockSpec((tm, D), lambda i, k: (i, 0)),
        scratch_shapes=[pltpu.VMEM((tm, D), F32)],
        compiler_params=_cp(("parallel", "arbitrary")),
        name="matmul_norm_residual",
    )(a, w, h, nw.reshape(1, D))


def _ple_kernel(h_ref, wg_ref, bg_ref, p_ref, wp_ref, o_ref, hb_ref, *, tn):
    j = pl.program_id(1)

    @pl.when(j == 0)
    def _():
        hb_ref[...] = h_ref[...].astype(BF16)

    g = jnp.dot(hb_ref[...], wg_ref[...], preferred_element_type=F32) + bg_ref[...]
    pp = jnp.dot(p_ref[...].astype(BF16), wp_ref[...], preferred_element_type=F32)
    hs = h_ref[:, pl.ds(pl.multiple_of(j * tn, tn), tn)]
    o_ref[...] = hs + jax.nn.sigmoid(g) * pp


def ple(h, wg, bg, p, wp, layer, *, tm=1024, tn=1024):
    T, D = h.shape
    P = p.shape[2]
    tm = _row_tile(T, tm)
    return pl.pallas_call(
        functools.partial(_ple_kernel, tn=tn),
        out_shape=jax.ShapeDtypeStruct((T, D), F32),
        grid=(T // tm, D // tn),
        in_specs=[pl.BlockSpec((tm, D), lambda i, j: (i, 0)),
                  pl.BlockSpec((None, D, tn), lambda i, j: (layer, 0, j)),
                  pl.BlockSpec((1, tn), lambda i, j: (0, j)),
                  pl.BlockSpec((None, tm, P), lambda i, j: (layer, i, 0)),
                  pl.BlockSpec((None, P, tn), lambda i, j: (layer, 0, j))],
        out_specs=pl.BlockSpec((tm, tn), lambda i, j: (i, j)),
        scratch_shapes=[pltpu.VMEM((tm, D), BF16)],
        compiler_params=_cp(("parallel", "arbitrary")),
        name="ple",
    )(h, wg, bg.reshape(1, D), p, wp)


def _split3(x):
    h1 = x.astype(BF16)
    r1 = x - h1.astype(F32)
    h2 = r1.astype(BF16)
    h3 = (r1 - h2.astype(F32)).astype(BF16)
    return h1, h2, h3


GLA_HEADS_PER_STEP = 4


def _gla_kernel(qk_ref, v_ref, g_ref, alr_ref, wa_ref, ba_ref, nw_ref, o_ref, st_ref, *, tr):
    L = GLA_CHUNK

    @pl.when(pl.program_id(2) == 0)
    def _():
        st_ref[...] = jnp.zeros_like(st_ref)

    row = lax.broadcasted_iota(jnp.int32, (tr, tr), 0)
    col = lax.broadcasted_iota(jnp.int32, (tr, tr), 1)
    same = lax.shift_right_logical(row, 6) == lax.shift_right_logical(col, 6)
    causal = same & (row >= col)
    tri = jnp.where(causal, 1.0, 0.0).astype(BF16)
    ones = jnp.where(same, 1.0, 0.0).astype(BF16)
    lo = lax.broadcasted_iota(jnp.int32, (1, LANE), 1) < GLA_DK
    tn = (((0,), (0,)), ((), ()))
    alr = alr_ref[0]

    for hh in range(GLA_HEADS_PER_STEP):
        lanes = slice(hh * LANE, (hh + 1) * LANE)
        x = jnp.dot(alr, wa_ref[hh], preferred_element_type=F32) + ba_ref[hh]
        log_a = (jnp.minimum(x, 0.0) - jnp.log1p(jnp.exp(-jnp.abs(x)))) * (1.0 / GLA_TAU)
        parts = _split3(log_a)
        b = functools.reduce(jnp.add, [jnp.dot(tri, a, preferred_element_type=F32) for a in parts])
        b_end = functools.reduce(jnp.add, [jnp.dot(ones, a, preferred_element_type=F32) for a in parts])
        blk = qk_ref[0, :, lanes].astype(F32)
        swp = pltpu.roll(blk, 64, 1)
        q_dec = jnp.where(lo, blk * jnp.exp(b) * (GLA_DK ** -0.5), 0.0).astype(BF16)
        k_inv = jnp.where(lo, swp * jnp.exp(-b), 0.0).astype(BF16)
        k_end = jnp.where(lo, swp * jnp.exp(b_end - b), 0.0).astype(BF16)
        decay = jnp.exp(b_end)
        v = v_ref[0, :, lanes]
        attn = lax.dot_general(q_dec, k_inv, _NT, preferred_element_type=F32)
        attn = jnp.where(causal, attn, 0.0).astype(BF16)
        o_intra = jnp.dot(attn, v, preferred_element_type=F32)

        st = st_ref[hh]
        outs = []
        for c in range(tr // L):
            rows = slice(c * L, (c + 1) * L)
            outs.append(o_intra[rows] + lax.dot_general(q_dec[rows], st.astype(BF16), _NT,
                                                        preferred_element_type=F32))
            st = st * decay[c * L:c * L + 1] + lax.dot_general(v[rows], k_end[rows], tn,
                                                               preferred_element_type=F32)
        st_ref[hh] = st
        o = jnp.concatenate(outs, axis=0)
        g = g_ref[0, :, lanes].astype(F32)
        o_ref[0, :, lanes] = (_rms(o, nw_ref[...]) * (g * jax.nn.sigmoid(g))).astype(o_ref.dtype)


def gla(z3, wa, ba, nw, *, tr=256):
    B, S, _ = z3.shape
    tr = min(tr, S)
    H = GLA_HEADS
    hp = GLA_HEADS_PER_STEP
    wide = hp * LANE
    return pl.pallas_call(
        functools.partial(_gla_kernel, tr=tr),
        out_shape=jax.ShapeDtypeStruct((B, S, H * GLA_DV), BF16),
        grid=(B, H // hp, S // tr),
        in_specs=[pl.BlockSpec((1, tr, wide), lambda b, h, r: (b, r, AB_BLK_QK // hp + h)),
                  pl.BlockSpec((1, tr, wide), lambda b, h, r: (b, r, AB_BLK_V // hp + h)),
                  pl.BlockSpec((1, tr, wide), lambda b, h, r: (b, r, AB_BLK_G // hp + h)),
                  pl.BlockSpec((1, tr, LANE), lambda b, h, r: (b, r, AB_BLK_ALR)),
                  pl.BlockSpec((hp, LANE, LANE), lambda b, h, r: (h, 0, 0)),
                  pl.BlockSpec((hp, 1, LANE), lambda b, h, r: (h, 0, 0)),
                  pl.BlockSpec((1, LANE), lambda b, h, r: (0, 0))],
        out_specs=pl.BlockSpec((1, tr, wide), lambda b, h, r: (b, r, h)),
        scratch_shapes=[pltpu.VMEM((hp, GLA_DV, LANE), F32)],
        compiler_params=_cp(("parallel", "parallel", "arbitrary")),
        name="gla",
    )(z3, z3, z3, z3, wa, ba, nw.reshape(1, GLA_DV))


def _mla_proj_kernel(cq_ref, ckv_ref, kr_ref, qnw_ref, kvnw_ref, wq_ref, wkv_ref, c2_ref, s2_ref,
                     q_ref, k_ref, v_ref, cqn_ref, ckvn_ref, kro_ref):
    @pl.when(pl.program_id(1) == 0)
    def _():
        cqn_ref[...] = _rms(cq_ref[...].astype(F32), qnw_ref[...]).astype(BF16)
        ckvn_ref[...] = _rms(ckv_ref[...].astype(F32), kvnw_ref[...]).astype(BF16)
        kr = kr_ref[...].astype(F32)
        kro_ref[...] = (kr * c2_ref[...] + pltpu.roll(kr, 64, 1) * s2_ref[...]).astype(BF16)

    scale = MLA_QK ** -0.5 * LOG2E
    yq = jnp.dot(cqn_ref[...], wq_ref[0], preferred_element_type=F32)
    y2 = yq[:, LANE:]
    qr = y2 * c2_ref[...] + pltpu.roll(y2, 64, 1) * s2_ref[...]
    q_ref[0, 0, :, :MLA_NOPE] = (yq[:, :LANE] * scale).astype(BF16)
    q_ref[0, 0, :, MLA_NOPE:] = (qr[:, :MLA_ROPE] * scale).astype(BF16)
    ykv = jnp.dot(ckvn_ref[...], wkv_ref[0], preferred_element_type=F32)
    k_ref[0, 0, :, :MLA_NOPE] = ykv[:, :LANE].astype(BF16)
    k_ref[0, 0, :, MLA_NOPE:] = kro_ref[:, :MLA_ROPE]
    v_ref[0, 0] = ykv[:, LANE:].astype(BF16)


def mla_proj(z, qnw, kvnw, wq, wkv, c2, s2, B, S, *, tm=512):
    T = z.shape[0]
    tm = min(tm, S)
    H = MLA_HEADS
    nb = S // tm

    def omap(i, h):
        return (i // nb, h, i % nb, 0)

    return pl.pallas_call(
        _mla_proj_kernel,
        out_shape=[jax.ShapeDtypeStruct((B, H, S, MLA_QK), BF16),
                   jax.ShapeDtypeStruct((B, H, S, MLA_QK), BF16),
                   jax.ShapeDtypeStruct((B, H, S, MLA_V), BF16)],
        grid=(T // tm, H),
        in_specs=[pl.BlockSpec((tm, MLA_Q_RANK), lambda i, h: (i, AB_BLK_CQ // 4)),
                  pl.BlockSpec((tm, MLA_KV_RANK), lambda i, h: (i, AB_BLK_CKV // 4)),
                  pl.BlockSpec((tm, LANE), lambda i, h: (i, AB_BLK_KR)),
                  pl.BlockSpec((1, MLA_Q_RANK), lambda i, h: (0, 0)),
                  pl.BlockSpec((1, MLA_KV_RANK), lambda i, h: (0, 0)),
                  pl.BlockSpec((1, MLA_Q_RANK, 2 * LANE), lambda i, h: (h, 0, 0)),
                  pl.BlockSpec((1, MLA_KV_RANK, 2 * LANE), lambda i, h: (h, 0, 0)),
                  pl.BlockSpec((tm, LANE), lambda i, h: (i, 0)),
                  pl.BlockSpec((tm, LANE), lambda i, h: (i, 0))],
        out_specs=[pl.BlockSpec((1, 1, tm, MLA_QK), omap),
                   pl.BlockSpec((1, 1, tm, MLA_QK), omap),
                   pl.BlockSpec((1, 1, tm, MLA_V), omap)],
        scratch_shapes=[pltpu.VMEM((tm, MLA_Q_RANK), BF16), pltpu.VMEM((tm, MLA_KV_RANK), BF16),
                        pltpu.VMEM((tm, LANE), BF16)],
        compiler_params=_cp(("parallel", "arbitrary")),
        name="mla_proj",
    )(z, z, z, qnw.reshape(1, -1), kvnw.reshape(1, -1), wq, wkv, c2, s2)


_NT = (((1,), (1,)), ((), ()))


def _softmax_tile(s, v, m_ref, l_ref, acc_ref, first):
    M, tk = s.shape
    chunks = [s[:, c * LANE:(c + 1) * LANE] for c in range(tk // LANE)]
    mrow = jnp.max(functools.reduce(jnp.maximum, chunks), axis=-1, keepdims=True)
    if first:
        m_new = jnp.broadcast_to(mrow, (M, LANE))
    else:
        m_prev = m_ref[...]
        m_new = jnp.maximum(m_prev, mrow)
    ps = [jnp.exp2(c - m_new) for c in chunks]
    lsum = functools.reduce(jnp.add, ps)
    p = (jnp.concatenate(ps, axis=1) if len(ps) > 1 else ps[0]).astype(BF16)
    pv = jnp.dot(p, v, preferred_element_type=F32)
    if first:
        l_ref[...] = lsum
        acc_ref[...] = pv
    else:
        alpha = jnp.exp2(m_prev - m_new)
        l_ref[...] = alpha * l_ref[...] + lsum
        acc_ref[...] = alpha * acc_ref[...] + pv
    m_ref[...] = m_new


def _softmax_finish(l_ref, acc_ref):
    return acc_ref[...] / jnp.sum(l_ref[...], axis=-1, keepdims=True)


def _mla_attn_kernel(q_ref, k_ref, v_ref, o_ref, m_ref, l_ref, acc_ref, *, tq, tk):
    qi = pl.program_id(2)
    r = tq // tk
    q = q_ref[0, 0]
    row = lax.broadcasted_iota(jnp.int32, (tq, tk), 0)
    col = lax.broadcasted_iota(jnp.int32, (tq, tk), 1)

    def score(j, d):
        off = pl.multiple_of(j * tk, tk)
        s = lax.dot_general(q, k_ref[0, 0, pl.ds(off, tk), :], _NT, preferred_element_type=F32)
        if d is not None:
            s = jnp.where(col + d * tk <= row, s, NEG)
        return s, v_ref[0, 0, pl.ds(off, tk), :]

    def attend(tiles, first):
        for n, (s, v) in enumerate(tiles):
            _softmax_tile(s, v, m_ref, l_ref, acc_ref, first and n == 0)

    attend([score(qi * r + d, d) for d in range(r)], True)

    def body(i, carry):
        attend([score(2 * i, None), score(2 * i + 1, None)], False)
        return carry

    lax.fori_loop(0, qi * (r // 2), body, 0)
    o_ref[0] = _softmax_finish(l_ref, acc_ref).astype(o_ref.dtype)


def mla_attention(q, k, v, *, tq=512, tk=256):
    B, H, S, _ = q.shape
    tq = min(tq, S)
    tk = min(tk, tq // 2)
    assert tq % (2 * tk) == 0
    return pl.pallas_call(
        functools.partial(_mla_attn_kernel, tq=tq, tk=tk),
        out_shape=jax.ShapeDtypeStruct((B, S, H * MLA_V), BF16),
        grid=(B, H, S // tq),
        in_specs=[pl.BlockSpec((1, 1, tq, MLA_QK), lambda b, h, i: (b, h, i, 0)),
                  pl.BlockSpec((1, 1, S, MLA_QK), lambda b, h, i: (b, h, 0, 0)),
                  pl.BlockSpec((1, 1, S, MLA_V), lambda b, h, i: (b, h, 0, 0))],
        out_specs=pl.BlockSpec((1, tq, MLA_V), lambda b, h, i: (b, i, h)),
        scratch_shapes=[pltpu.VMEM((tq, LANE), F32), pltpu.VMEM((tq, LANE), F32),
                        pltpu.VMEM((tq, MLA_V), F32)],
        compiler_params=_cp(("parallel", "parallel", "arbitrary")),
        name="mla_attention",
    )(q, k, v)


def _nsa_compress_kernel(r_ref, pos_ref, w1_ref, w2_ref, o_ref):
    r = r_ref[0, 0, 0]
    half = r.shape[1]
    nr = r.shape[0]
    a = jnp.dot(r, w1_ref[0, :half, :], preferred_element_type=F32)
    b = jnp.dot(r, w1_ref[0, half:, :], preferred_element_type=F32)
    pos = jnp.broadcast_to(pos_ref[0], (8, 2 * half)).astype(BF16)
    c = jnp.dot(pos, w1_ref[0], preferred_element_type=F32)[0:1, :]
    pre = a + pltpu.roll(b, nr - 1, 0) + c
    o_ref[0, 0, 0] = jnp.dot(jax.nn.gelu(pre).astype(BF16), w2_ref[0],
                             preferred_element_type=F32).astype(o_ref.dtype)


def nsa_compress(r, pos, w1, w2):
    _, B, G, NR, W = r.shape
    dh = NSA_HEAD_DIM
    return pl.pallas_call(
        _nsa_compress_kernel,
        out_shape=jax.ShapeDtypeStruct((2, B, G, NR, dh), BF16),
        grid=(2, B, G),
        in_specs=[pl.BlockSpec((1, 1, 1, NR, W), lambda c, b, g: (c, b, g, 0, 0)),
                  pl.BlockSpec((1, 1, 2 * W), lambda c, b, g: (c, 0, 0)),
                  pl.BlockSpec((1, 2 * W, dh), lambda c, b, g: (c, 0, 0)),
                  pl.BlockSpec((1, dh, dh), lambda c, b, g: (c, 0, 0))],
        out_specs=pl.BlockSpec((1, 1, 1, NR, dh), lambda c, b, g: (c, b, g, 0, 0)),
        compiler_params=_cp(("parallel", "parallel", "parallel")),
        name="nsa_compress",
    )(r, pos, w1, w2)


def _nsa_cmp_kernel(q_ref, kc_ref, vc_ref, ovt_ref, o_ref, sel_ref, *, tq, ns, ksel):
    qi = pl.program_id(2)
    nc = kc_ref.shape[3]
    kc = kc_ref[0, 0, 0]
    vc = vc_ref[0, 0, 0]
    t = qi * tq + lax.broadcasted_iota(jnp.int32, (tq, 1), 0)
    n = lax.broadcasted_iota(jnp.int32, (1, nc), 1)
    ok = (n * NSA_CMP_STRIDE + (NSA_CMP_LEN - 1)) <= t
    nt = (((1,), (1,)), ((), ()))
    psum = jnp.zeros((tq, nc), F32)
    for j in range(NSA_HPG):
        q = q_ref[0, :, j * LANE:(j + 1) * LANE]
        s = lax.dot_general(q, kc, nt, preferred_element_type=F32)
        s = jnp.where(ok, s, NEG)
        e = jnp.where(ok, jnp.exp2(s - jnp.max(s, axis=-1, keepdims=True)), 0.0)
        d = jnp.sum(e, axis=-1, keepdims=True)
        p = e * jnp.where(d > 0.0, 1.0 / d, 0.0)
        o_ref[0, :, j * LANE:(j + 1) * LANE] = jnp.dot(p.astype(BF16), vc,
                                                        preferred_element_type=F32).astype(o_ref.dtype)
        psum = psum + p
    ph = psum.astype(BF16)
    plo = (psum - ph.astype(F32)).astype(BF16)
    ovt = ovt_ref[...]
    imp = (lax.dot_general(ovt, ph, nt, preferred_element_type=F32)
           + lax.dot_general(ovt, plo, nt, preferred_element_type=F32))
    m = lax.broadcasted_iota(jnp.int32, (ns, 1), 0)
    tt = qi * tq + lax.broadcasted_iota(jnp.int32, (1, tq), 1)
    causal = m * NSA_SEL_LEN <= tt
    cur = lax.shift_right_logical(tt, 6)
    forced = (m == 0) | (m == cur) | (m == cur - 1)
    score = jnp.where(causal, jnp.where(forced, FORCE, imp), -FORCE)
    cnt = jnp.zeros((ns, tq), F32)
    for m2 in range(ns):
        r = score[m2:m2 + 1, :]
        ahead = (r > score) | ((r == score) & (m2 < m))
        cnt = cnt + jnp.where(ahead, 1.0, 0.0)
    sel = jnp.where((cnt < float(ksel)) & causal, 1.0, 0.0)
    selp = jnp.concatenate([sel, jnp.zeros((LANE - ns, tq), F32)], axis=0)
    sel_ref[0, 0] = selp.T.astype(sel_ref.dtype)


def nsa_cmp_select(z3, kv_cmp, ovt, *, tq=512):
    B, S, _ = z3.shape
    G = NSA_KV_GROUPS
    tq = min(tq, S)
    ns = S // NSA_SEL_LEN
    nc = kv_cmp.shape[3]
    kern = functools.partial(_nsa_cmp_kernel, tq=tq, ns=ns, ksel=min(NSA_SEL_TOPK, ns))
    return pl.pallas_call(
        kern,
        out_shape=[jax.ShapeDtypeStruct((B, S, NSA_HEADS * NSA_HEAD_DIM), BF16),
                   jax.ShapeDtypeStruct((B, G, S, LANE), BF16)],
        grid=(B, G, S // tq),
        in_specs=[pl.BlockSpec((1, tq, 4 * LANE), lambda b, g, i: (b, i, g)),
                  pl.BlockSpec((1, 1, 1, nc, LANE), lambda b, g, i: (0, b, g, 0, 0)),
                  pl.BlockSpec((1, 1, 1, nc, LANE), lambda b, g, i: (1, b, g, 0, 0)),
                  pl.BlockSpec((ns, nc), lambda b, g, i: (0, 0))],
        out_specs=[pl.BlockSpec((1, tq, 4 * LANE), lambda b, g, i: (b, i, g)),
                   pl.BlockSpec((1, 1, tq, LANE), lambda b, g, i: (b, g, i, 0))],
        compiler_params=_cp(("parallel", "parallel", "parallel")),
        name="nsa_cmp_select",
    )(z3, kv_cmp, kv_cmp, ovt)


def _stack_heads(q_ref, q4_ref, t):
    for h in range(NSA_HPG):
        q4_ref[h * t:(h + 1) * t, :] = q_ref[0, :, h * LANE:(h + 1) * LANE]


def _mask_heads(s, mask, t):
    return jnp.concatenate([jnp.where(mask, s[h * t:(h + 1) * t], NEG) for h in range(NSA_HPG)], axis=0)


def _nsa_sel_kernel(q_ref, k_ref, v_ref, sel_ref, e_ref, o_ref, q4_ref, m_ref, l_ref, acc_ref, *, t):
    qi = pl.program_id(2)
    _stack_heads(q_ref, q4_ref, t)
    sel = sel_ref[0, 0]
    row = lax.broadcasted_iota(jnp.int32, (t, t), 0)
    col = lax.broadcasted_iota(jnp.int32, (t, t), 1)

    def score(j, diag):
        off = pl.multiple_of(j * t, t)
        mask = jnp.dot(sel, e_ref[j], preferred_element_type=F32) > 0.5
        if diag:
            mask = mask & (col <= row)
        s = lax.dot_general(q4_ref[...], k_ref[0, pl.ds(off, t), :], _NT, preferred_element_type=F32)
        return _mask_heads(s, mask, t), v_ref[0, pl.ds(off, t), :]

    def attend(tiles, first):
        for n, (s, v) in enumerate(tiles):
            _softmax_tile(s, v, m_ref, l_ref, acc_ref, first and n == 0)

    attend([score(qi, True)], True)
    odd = qi & 1

    @pl.when(odd == 1)
    def _():
        attend([score(0, False)], False)

    def body(i, carry):
        j = odd + 2 * i
        attend([score(j, False), score(j + 1, False)], False)
        return carry

    lax.fori_loop(0, lax.shift_right_logical(qi, 1), body, 0)
    o = _softmax_finish(l_ref, acc_ref)
    for h in range(NSA_HPG):
        o_ref[0, :, h * LANE:(h + 1) * LANE] = o[h * t:(h + 1) * t].astype(o_ref.dtype)


def _expand_blocks(S, t):
    key = np.arange(S).reshape(S // t, 1, t)
    blk = np.arange(LANE).reshape(1, LANE, 1)
    return jnp.asarray((key // NSA_SEL_LEN == blk).astype(np.float32), dtype=BF16)


def nsa_selected(z3, sel, *, t=256):
    B, S, _ = z3.shape
    G = NSA_KV_GROUPS
    t = min(t, S)
    n = S // t
    big = pl.BlockSpec((1, t, 4 * LANE), lambda b, g, i: (b, i, g))
    return pl.pallas_call(
        functools.partial(_nsa_sel_kernel, t=t),
        out_shape=jax.ShapeDtypeStruct((B, S, NSA_HEADS * NSA_HEAD_DIM), BF16),
        grid=(B, G, n),
        in_specs=[big,
                  pl.BlockSpec((1, S, LANE), lambda b, g, i: (b, 0, NSA_BLK_KS + g)),
                  pl.BlockSpec((1, S, LANE), lambda b, g, i: (b, 0, NSA_BLK_VS + g)),
                  pl.BlockSpec((1, 1, t, LANE), lambda b, g, i: (b, g, i, 0)),
                  pl.BlockSpec((n, LANE, t), lambda b, g, i: (0, 0, 0))],
        out_specs=big,
        scratch_shapes=[pltpu.VMEM((NSA_HPG * t, LANE), BF16), pltpu.VMEM((NSA_HPG * t, LANE), F32),
                        pltpu.VMEM((NSA_HPG * t, LANE), F32), pltpu.VMEM((NSA_HPG * t, LANE), F32)],
        compiler_params=_cp(("parallel", "parallel", "arbitrary")),
        name="nsa_selected",
    )(z3, z3, z3, sel, _expand_blocks(S, t))


def _nsa_win_kernel(q_ref, k_ref, v_ref, oc_ref, os_ref, gl_ref, bg_ref, o_ref, q4_ref, m_ref, l_ref, acc_ref,
                    *, t, nw):
    qi = pl.program_id(2)
    _stack_heads(q_ref, q4_ref, t)
    row = lax.broadcasted_iota(jnp.int32, (t, t), 0)
    col = lax.broadcasted_iota(jnp.int32, (t, t), 1)

    def score(w, mask):
        off = pl.multiple_of((qi - w) * t, t)
        s = lax.dot_general(q4_ref[...], k_ref[0, pl.ds(off, t), :], _NT, preferred_element_type=F32)
        if mask is not None:
            s = _mask_heads(s, mask, t)
        return s, v_ref[0, pl.ds(off, t), :]

    def attend(tiles, first):
        for n, (s, v) in enumerate(tiles):
            _softmax_tile(s, v, m_ref, l_ref, acc_ref, first and n == 0)

    def older(w):
        return score(w, col > row + (w * t - NSA_WINDOW) if (w + 1) * t > NSA_WINDOW else None)

    attend([score(0, col <= row)], True)
    for w in range(1, nw):
        @pl.when(jnp.minimum(qi, nw - 1) == w)
        def _(w=w):
            attend([older(u) for u in range(1, w + 1)], False)

    o_win = _softmax_finish(l_ref, acc_ref)
    gates = jax.nn.sigmoid(gl_ref[0].astype(F32) + bg_ref[0])
    for h in range(NSA_HPG):
        cols = slice(h * LANE, (h + 1) * LANE)
        o = (gates[:, h:h + 1] * oc_ref[0, :, cols].astype(F32)
             + gates[:, 4 + h:5 + h] * os_ref[0, :, cols].astype(F32)
             + gates[:, 8 + h:9 + h] * o_win[h * t:(h + 1) * t])
        o_ref[0, :, cols] = o.astype(o_ref.dtype)


def nsa_window_merge(z3, o_cmp, o_sel, bg, *, t=256):
    B, S, _ = z3.shape
    G = NSA_KV_GROUPS
    t = min(t, S)
    n = S // t
    assert NSA_WINDOW % t == 0
    nw = min(NSA_WINDOW // t + 1, n)
    big = pl.BlockSpec((1, t, 4 * LANE), lambda b, g, i: (b, i, g))
    return pl.pallas_call(
        functools.partial(_nsa_win_kernel, t=t, nw=nw),
        out_shape=jax.ShapeDtypeStruct((B, S, NSA_HEADS * NSA_HEAD_DIM), BF16),
        grid=(B, G, n),
        in_specs=[big,
                  pl.BlockSpec((1, S, LANE), lambda b, g, i: (b, 0, NSA_BLK_KW + g)),
                  pl.BlockSpec((1, S, LANE), lambda b, g, i: (b, 0, NSA_BLK_VW + g)),
                  big, big,
                  pl.BlockSpec((1, t, LANE), lambda b, g, i: (b, i, NSA_BLK_GATE + g)),
                  pl.BlockSpec((1, 1, LANE), lambda b, g, i: (g, 0, 0))],
        out_specs=big,
        scratch_shapes=[pltpu.VMEM((NSA_HPG * t, LANE), BF16), pltpu.VMEM((NSA_HPG * t, LANE), F32),
                        pltpu.VMEM((NSA_HPG * t, LANE), F32), pltpu.VMEM((NSA_HPG * t, LANE), F32)],
        compiler_params=_cp(("parallel", "parallel", "arbitrary")),
        name="nsa_window_merge",
    )(z3, z3, z3, o_cmp, o_sel, z3, bg)


def _rot_half_cols(w):
    half = w.shape[-1] // 2
    return jnp.concatenate([-w[..., half:], w[..., :half]], axis=-1)


def _prep_ab(w_in, w_alpha_up, b_alpha, w_uq, w_ukv):
    D = w_in.shape[0]
    q_g, k_g, v_g, g_g, a_lr, c_q, c_kv, k_r = _split_cols(w_in, AB_SPLITS)
    qk = jnp.concatenate([q_g.reshape(D, GLA_HEADS, GLA_DK), k_g.reshape(D, GLA_HEADS, GLA_DK)],
                         axis=-1).reshape(D, GLA_HEADS * LANE)
    tail = jnp.concatenate([k_r, _rot_half_cols(k_r), a_lr,
                            jnp.zeros((D, 512 - 2 * MLA_ROPE - GLA_GATE_RANK), w_in.dtype)], axis=-1)
    w = jnp.concatenate([qk, v_g, g_g, c_q, c_kv, tail], axis=-1).astype(BF16)
    wa = w_alpha_up.reshape(GLA_GATE_RANK, GLA_HEADS, GLA_DK).transpose(1, 0, 2)
    wa = jnp.concatenate([wa, wa], axis=-1)
    wa = jnp.pad(wa, ((0, 0), (0, LANE - GLA_GATE_RANK), (0, 0))).astype(BF16)
    ba = b_alpha.reshape(GLA_HEADS, 1, GLA_DK)
    ba = jnp.concatenate([ba, ba], axis=-1)
    wq = w_uq.reshape(MLA_Q_RANK, MLA_HEADS, MLA_QK)
    rope = wq[..., MLA_NOPE:]
    wq = jnp.concatenate([wq, _rot_half_cols(rope)], axis=-1).transpose(1, 0, 2).astype(BF16)
    wkv = w_ukv.reshape(MLA_KV_RANK, MLA_HEADS, MLA_NOPE + MLA_V).transpose(1, 0, 2).astype(BF16)
    return w, wa, ba, wq, wkv


def _prep_nsa(w_in, b_gate):
    D = w_in.shape[0]
    q, kc, vc, ks, vs, kw, vw, gl = _split_cols(w_in, NSA_SPLITS)
    G, HG = NSA_KV_GROUPS, NSA_HPG
    glp = gl.reshape(D, G, HG, 3).transpose(0, 1, 3, 2).reshape(D, G, 3 * HG)
    glp = jnp.pad(glp, ((0, 0), (0, 0), (0, LANE - 3 * HG))).reshape(D, G * LANE)
    w = jnp.concatenate([q, kc, ks, kw, vc, vs, vw, glp], axis=-1).astype(BF16)
    bg = b_gate.reshape(G, HG, 3).transpose(0, 2, 1).reshape(G, 1, 3 * HG)
    bg = jnp.pad(bg, ((0, 0), (0, 0), (0, LANE - 3 * HG)))
    return w, bg


def _overlap_t(S):
    nr = S // NSA_CMP_STRIDE
    ns = S // NSA_SEL_LEN
    c_start = np.arange(nr) * NSA_CMP_STRIDE
    c_end = c_start + NSA_CMP_LEN
    s_start = np.arange(ns) * NSA_SEL_LEN
    s_end = s_start + NSA_SEL_LEN
    ov = (c_start[None, :] < s_end[:, None]) & (c_end[None, :] > s_start[:, None])
    return jnp.asarray(ov.astype(np.float32), dtype=BF16)


def gla_mla_mixer(h, pre_w, B, S, c2, s2, w_in, w_alpha_up, b_alpha, gla_norm_w,
                  q_norm_w, w_uq, kv_norm_w, w_ukv):
    w, wa, ba, wq, wkv = _prep_ab(w_in, w_alpha_up, b_alpha, w_uq, w_ukv)
    z = norm_matmul(h, pre_w, w)
    z3 = z.reshape(B, S, AB_Z)
    o_gla = gla(z3, wa, ba, gla_norm_w)
    qm, km, vm = mla_proj(z, q_norm_w, kv_norm_w, wq, wkv, c2, s2, B, S)
    o_mla = mla_attention(qm, km, vm)
    return [o_gla.reshape(B * S, -1), o_mla.reshape(B * S, -1)]


def nsa_mixer(h, pre_w, B, S, c128, s128, w_in, b_gate, cmp_pos, cmp_w1, cmp_w2):
    w, bg = _prep_nsa(w_in, b_gate)
    z = norm_matmul_rope(h, pre_w, w, c128, s128)
    z3 = z.reshape(B, S, NSA_Z)
    G, dh = NSA_KV_GROUPS, NSA_HEAD_DIM

    def blocks(blk):
        t = z3[:, :, blk * LANE:(blk + G) * LANE].reshape(B, S, G, dh).transpose(0, 2, 1, 3)
        return t.reshape(B, G, S // NSA_CMP_STRIDE, NSA_CMP_STRIDE * dh)

    r = jnp.stack([blocks(NSA_BLK_KC), blocks(NSA_BLK_VC)])
    kv_cmp = nsa_compress(r, cmp_pos.reshape(2, 1, NSA_CMP_LEN * dh),
                          cmp_w1.astype(BF16), cmp_w2.astype(BF16))
    o_cmp, sel = nsa_cmp_select(z3, kv_cmp, _overlap_t(S))
    o_sel = nsa_selected(z3, sel)
    o = nsa_window_merge(z3, o_cmp, o_sel, bg)
    return [o.reshape(B * S, -1)]


def kernel(x, p, positions, ln_mix_pre, ln_mix_post, ln_ffn_pre, ln_ffn_post, ab_w_in, gla_w_alpha_up, gla_b_alpha, gla_norm_w, mla_q_norm_w, mla_w_uq, mla_kv_norm_w, mla_w_ukv, ab_w_out, nsa_w_in, nsa_b_gate, nsa_cmp_pos, nsa_cmp_w1, nsa_cmp_w2, nsa_w_out, ffn_w_gate, ffn_w_up, ffn_w_down, ple_w_gate, ple_b_gate, ple_w_proj):
    B, S, D = x.shape
    T = B * S
    depth = p.shape[0]
    c2, s2, c128, s128 = rope_tables(positions)
    h = x.reshape(T, D)
    ab_out, nsa_out = ab_w_out.astype(BF16), nsa_w_out.astype(BF16)
    w_gate, w_up, w_down = ffn_w_gate.astype(BF16), ffn_w_up.astype(BF16), ffn_w_down.astype(BF16)
    ple_gate, ple_proj = ple_w_gate.astype(BF16), ple_w_proj.astype(BF16)
    p3 = p.reshape(depth, T, -1)
    for i in range(depth):
        j = i // 2
        if i % 2 == 0:
            mix = gla_mla_mixer(h, ln_mix_pre[i], B, S, c2, s2, ab_w_in[j], gla_w_alpha_up[j],
                                gla_b_alpha[j], gla_norm_w[j], mla_q_norm_w[j], mla_w_uq[j],
                                mla_kv_norm_w[j], mla_w_ukv[j])
            w_out = ab_out
        else:
            mix = nsa_mixer(h, ln_mix_pre[i], B, S, c128, s128, nsa_w_in[j], nsa_b_gate[j],
                            nsa_cmp_pos[j], nsa_cmp_w1[j], nsa_cmp_w2[j])
            w_out = nsa_out
        h = matmul_parts_norm_residual(mix, w_out, j, h, ln_mix_post[i])
        act = norm_swiglu(h, ln_ffn_pre[i], w_gate, w_up, i)
        h = matmul_norm_residual(act, w_down, i, h, ln_ffn_post[i], tm=1024, tk=512)
        h = ple(h, ple_gate, ple_b_gate[i], p3, ple_proj, i)
    return h.reshape(B, S, D)
```

```python
import functools

import numpy as np
import jax
import jax.numpy as jnp
from jax import lax
from jax.experimental import pallas as pl
from jax.experimental.pallas import tpu as pltpu

F32 = jnp.float32
BF16 = jnp.bfloat16

D_MODEL = 2048
PLE_DIM = 256
ROPE_THETA = 10000.0
NORM_EPS = 1e-6
NEG = -1e30
FORCE = 1e6

GLA_HEADS = 8
GLA_DK = 64
GLA_DV = 128
GLA_GATE_RANK = 16
GLA_TAU = 16.0
GLA_CHUNK = 64

MLA_HEADS = 8
MLA_Q_RANK = 512
MLA_KV_RANK = 512
MLA_NOPE = 128
MLA_ROPE = 64
MLA_V = 128
MLA_QK = MLA_NOPE + MLA_ROPE

NSA_HEADS = 16
NSA_KV_GROUPS = 4
NSA_HPG = NSA_HEADS // NSA_KV_GROUPS
NSA_HEAD_DIM = 128
NSA_CMP_LEN = 32
NSA_CMP_STRIDE = 16
NSA_SEL_LEN = 64
NSA_SEL_TOPK = 16
NSA_WINDOW = 512

FFN_HIDDEN = 5632

AB_SPLITS = (GLA_HEADS * GLA_DK, GLA_HEADS * GLA_DK, GLA_HEADS * GLA_DV, GLA_HEADS * GLA_DV,
             GLA_GATE_RANK, MLA_Q_RANK, MLA_KV_RANK, MLA_ROPE)
NSA_KV_W = NSA_KV_GROUPS * NSA_HEAD_DIM
NSA_SPLITS = (NSA_HEADS * NSA_HEAD_DIM,) + (NSA_KV_W,) * 6 + (3 * NSA_HEADS,)

LOG2E = 1.4426950408889634
LANE = 128
VMEM_LIMIT = 56 * 1024 * 1024

AB_Z = 4608
AB_BLK_QK = 0
AB_BLK_V = 8
AB_BLK_G = 16
AB_BLK_CQ = 24
AB_BLK_CKV = 28
AB_BLK_KR = 32
AB_BLK_ALR = 33

NSA_Z = 5632
NSA_BLK_Q = 0
NSA_BLK_KC = 16
NSA_BLK_KS = 20
NSA_BLK_KW = 24
NSA_BLK_VC = 28
NSA_BLK_VS = 32
NSA_BLK_VW = 36
NSA_BLK_GATE = 40
NSA_ROPE_TILES = 7
NSA_Q_TILES = 4


def _cp(sem):
    return pltpu.CompilerParams(dimension_semantics=sem, vmem_limit_bytes=VMEM_LIMIT)


def _rms(x, w):
    return x * lax.rsqrt(jnp.mean(x * x, axis=-1, keepdims=True) + NORM_EPS) * w


def _split_cols(z, widths):
    out, off = [], 0
    for w in widths:
        out.append(z[..., off:off + w])
        off += w
    return out


def _tables_kernel(pos_ref, inv_ref, c2_ref, s2_ref, c128_ref, s128_ref):
    pos = pos_ref[...].astype(F32)
    lane = lax.broadcasted_iota(jnp.int32, (1, LANE), 1)
    lo = lane < 64
    a64 = pos * inv_ref[0:1, :]
    c2_ref[...] = jnp.where(lo, jnp.cos(a64), 0.0)
    s2_ref[...] = jnp.where(lo, jnp.sin(a64), 0.0)
    a128 = pos * inv_ref[1:2, :]
    s = jnp.sin(a128)
    c128_ref[...] = jnp.cos(a128)
    s128_ref[...] = jnp.where(lo, -s, s)


def rope_tables(positions):
    T = positions.size
    inv32 = jnp.power(ROPE_THETA, -jnp.arange(0, MLA_ROPE, 2, dtype=F32) / MLA_ROPE)
    inv64 = jnp.power(ROPE_THETA, -jnp.arange(0, NSA_HEAD_DIM, 2, dtype=F32) / NSA_HEAD_DIM)
    inv = jnp.zeros((8, LANE), F32)
    inv = inv.at[0, :64].set(jnp.concatenate([inv32, inv32]))
    inv = inv.at[1, :].set(jnp.concatenate([inv64, inv64]))
    tm = min(T, 1024)
    spec = pl.BlockSpec((tm, LANE), lambda i: (i, 0))
    return pl.pallas_call(
        _tables_kernel,
        out_shape=[jax.ShapeDtypeStruct((T, LANE), F32)] * 4,
        grid=(T // tm,),
        in_specs=[pl.BlockSpec((tm, 1), lambda i: (i, 0)), pl.BlockSpec((8, LANE), lambda i: (0, 0))],
        out_specs=[spec] * 4,
        compiler_params=_cp(("parallel",)),
        name="rope_tables",
    )(positions.reshape(T, 1), inv)


def _norm_mm_kernel(x_ref, nw_ref, w_ref, o_ref, xn_ref):
    @pl.when(pl.program_id(1) == 0)
    def _():
        xn_ref[...] = _rms(x_ref[...], nw_ref[...]).astype(BF16)

    o_ref[...] = jnp.dot(xn_ref[...], w_ref[...], preferred_element_type=F32).astype(o_ref.dtype)


def _norm_mm_rope_kernel(x_ref, nw_ref, w_ref, cos_ref, sin_ref, o_ref, xn_ref, *, tn, scale):
    j = pl.program_id(1)

    @pl.when(j == 0)
    def _():
        xn_ref[...] = _rms(x_ref[...], nw_ref[...]).astype(BF16)

    mult = jnp.where(j < NSA_Q_TILES, scale, 1.0)
    is_rope = j < NSA_ROPE_TILES
    cos = jnp.where(is_rope, cos_ref[...] * mult, 1.0)
    sin = jnp.where(is_rope, sin_ref[...] * mult, 0.0)
    y = jnp.dot(xn_ref[...], w_ref[...], preferred_element_type=F32)
    for c in range(tn // LANE):
        seg = y[:, c * LANE:(c + 1) * LANE]
        o_ref[:, c * LANE:(c + 1) * LANE] = (seg * cos + pltpu.roll(seg, 64, 1) * sin).astype(o_ref.dtype)


def _norm_swiglu_kernel(x_ref, nw_ref, wg_ref, wu_ref, o_ref, xn_ref):
    @pl.when(pl.program_id(1) == 0)
    def _():
        xn_ref[...] = _rms(x_ref[...], nw_ref[...]).astype(BF16)

    xn = xn_ref[...]
    g = jnp.dot(xn, wg_ref[...], preferred_element_type=F32)
    u = jnp.dot(xn, wu_ref[...], preferred_element_type=F32)
    o_ref[...] = (g * jax.nn.sigmoid(g) * u).astype(o_ref.dtype)


def _row_tile(T, want):
    return min(T, want)


def norm_matmul(x, nw, w, *, tm=1024, tn=512):
    T, D = x.shape
    N = w.shape[1]
    tm = _row_tile(T, tm)
    return pl.pallas_call(
        _norm_mm_kernel,
        out_shape=jax.ShapeDtypeStruct((T, N), BF16),
        grid=(T // tm, N // tn),
        in_specs=[pl.BlockSpec((tm, D), lambda i, j: (i, 0)),
                  pl.BlockSpec((1, D), lambda i, j: (0, 0)),
                  pl.BlockSpec((D, tn), lambda i, j: (0, j))],
        out_specs=pl.BlockSpec((tm, tn), lambda i, j: (i, j)),
        scratch_shapes=[pltpu.VMEM((tm, D), BF16)],
        compiler_params=_cp(("parallel", "arbitrary")),
        name="norm_matmul",
    )(x, nw.reshape(1, D), w)


def norm_matmul_rope(x, nw, w, cos, sin, *, tm=1024, tn=512):
    T, D = x.shape
    N = w.shape[1]
    tm = _row_tile(T, tm)
    kern = functools.partial(_norm_mm_rope_kernel, tn=tn, scale=NSA_HEAD_DIM ** -0.5 * LOG2E)
    return pl.pallas_call(
        kern,
        out_shape=jax.ShapeDtypeStruct((T, N), BF16),
        grid=(T // tm, N // tn),
        in_specs=[pl.BlockSpec((tm, D), lambda i, j: (i, 0)),
                  pl.BlockSpec((1, D), lambda i, j: (0, 0)),
                  pl.BlockSpec((D, tn), lambda i, j: (0, j)),
                  pl.BlockSpec((tm, LANE), lambda i, j: (i, 0)),
                  pl.BlockSpec((tm, LANE), lambda i, j: (i, 0))],
        out_specs=pl.BlockSpec((tm, tn), lambda i, j: (i, j)),
        scratch_shapes=[pltpu.VMEM((tm, D), BF16)],
        compiler_params=_cp(("parallel", "arbitrary")),
        name="norm_matmul_rope",
    )(x, nw.reshape(1, D), w, cos, sin)


def norm_swiglu(x, nw, wg, wu, layer, *, tm=1024, tn=512):
    T, D = x.shape
    N = wg.shape[2]
    tm = _row_tile(T, tm)
    wspec = pl.BlockSpec((None, D, tn), lambda i, j: (layer, 0, j))
    return pl.pallas_call(
        _norm_swiglu_kernel,
        out_shape=jax.ShapeDtypeStruct((T, N), BF16),
        grid=(T // tm, N // tn),
        in_specs=[pl.BlockSpec((tm, D), lambda i, j: (i, 0)),
                  pl.BlockSpec((1, D), lambda i, j: (0, 0)),
                  wspec, wspec],
        out_specs=pl.BlockSpec((tm, tn), lambda i, j: (i, j)),
        scratch_shapes=[pltpu.VMEM((tm, D), BF16)],
        compiler_params=_cp(("parallel", "arbitrary")),
        name="norm_swiglu",
    )(x, nw.reshape(1, D), wg, wu)


def _mm_norm_res_kernel(a_ref, w_ref, h_ref, nw_ref, o_ref, acc_ref):
    k = pl.program_id(1)

    @pl.when(k == 0)
    def _():
        acc_ref[...] = jnp.zeros_like(acc_ref)

    acc_ref[...] += jnp.dot(a_ref[...], w_ref[...], preferred_element_type=F32)

    @pl.when(k == pl.num_programs(1) - 1)
    def _():
        o_ref[...] = h_ref[...] + _rms(acc_ref[...], nw_ref[...])


def _mm_norm_res_parts_kernel(*refs, widths):
    a_refs = refs[:len(widths)]
    w_ref, h_ref, nw_ref, o_ref = refs[len(widths):]
    m, off = None, 0
    for a_ref, wd in zip(a_refs, widths):
        part = jnp.dot(a_ref[...], w_ref[off:off + wd, :], preferred_element_type=F32)
        m = part if m is None else m + part
        off += wd
    o_ref[...] = h_ref[...] + _rms(m, nw_ref[...])


def matmul_parts_norm_residual(parts, w, layer, h, nw, *, tm=512):
    T = parts[0].shape[0]
    widths = tuple(a.shape[1] for a in parts)
    K, D = w.shape[1:]
    assert sum(widths) == K
    tm = _row_tile(T, tm)
    return pl.pallas_call(
        functools.partial(_mm_norm_res_parts_kernel, widths=widths),
        out_shape=jax.ShapeDtypeStruct((T, D), F32),
        grid=(T // tm,),
        in_specs=[pl.BlockSpec((tm, wd), lambda i: (i, 0)) for wd in widths]
        + [pl.BlockSpec((None, K, D), lambda i: (layer, 0, 0)),
           pl.BlockSpec((tm, D), lambda i: (i, 0)),
           pl.BlockSpec((1, D), lambda i: (0, 0))],
        out_specs=pl.BlockSpec((tm, D), lambda i: (i, 0)),
        compiler_params=_cp(("parallel",)),
        name="matmul_parts_norm_residual",
    )(*parts, w, h, nw.reshape(1, D))


def matmul_norm_residual(a, w, layer, h, nw, *, tm=512, tk=512):
    T, K = a.shape
    D = w.shape[2]
    tm = _row_tile(T, tm)
    return pl.pallas_call(
        _mm_norm_res_kernel,
        out_shape=jax.ShapeDtypeStruct((T, D), F32),
        grid=(T // tm, K // tk),
        in_specs=[pl.BlockSpec((tm, tk), lambda i, k: (i, k)),
                  pl.BlockSpec((None, tk, D), lambda i, k: (layer, k, 0)),
                  pl.BlockSpec((tm, D), lambda i, k: (i, 0)),
                  pl.BlockSpec((1, D), lambda i, k: (0, 0))],
        out_specs=pl.BlockSpec((tm, D), lambda i, k: (i, 0)),
        scratch_shapes=[pltpu.VMEM((tm, D), F32)],
        compiler_params=_cp(("parallel", "arbitrary")),
        name="matmul_norm_residual",
    )(a, w, h, nw.reshape(1, D))


def _ple_kernel(h_ref, wg_ref, bg_ref, p_ref, wp_ref, o_ref, hb_ref, *, tn):
    j = pl.program_id(1)

    @pl.when(j == 0)
    def _():
        hb_ref[...] = h_ref[...].astype(BF16)

    g = jnp.dot(hb_ref[...], wg_ref[...], preferred_element_type=F32) + bg_ref[...]
    pp = jnp.dot(p_ref[...].astype(BF16), wp_ref[...], preferred_element_type=F32)
    hs = h_ref[:, pl.ds(pl.multiple_of(j * tn, tn), tn)]
    o_ref[...] = hs + jax.nn.sigmoid(g) * pp


def ple(h, wg, bg, p, wp, layer, *, tm=1024, tn=1024):
    T, D = h.shape
    P = p.shape[2]
    tm = _row_tile(T, tm)
    return pl.pallas_call(
        functools.partial(_ple_kernel, tn=tn),
        out_shape=jax.ShapeDtypeStruct((T, D), F32),
        grid=(T // tm, D // tn),
        in_specs=[pl.BlockSpec((tm, D), lambda i, j: (i, 0)),
                  pl.BlockSpec((None, D, tn), lambda i, j: (layer, 0, j)),
                  pl.BlockSpec((1, tn), lambda i, j: (0, j)),
                  pl.BlockSpec((None, tm, P), lambda i, j: (layer, i, 0)),
                  pl.BlockSpec((None, P, tn), lambda i, j: (layer, 0, j))],
        out_specs=pl.BlockSpec((tm, tn), lambda i, j: (i, j)),
        scratch_shapes=[pltpu.VMEM((tm, D), BF16)],
        compiler_params=_cp(("parallel", "arbitrary")),
        name="ple",
    )(h, wg, bg.reshape(1, D), p, wp)


def _split3(x):
    h1 = x.astype(BF16)
    r1 = x - h1.astype(F32)
    h2 = r1.astype(BF16)
    h3 = (r1 - h2.astype(F32)).astype(BF16)
    return h1, h2, h3


GLA_HEADS_PER_STEP = 8


def _gla_kernel(qk_ref, v_ref, g_ref, alr_ref, wa_ref, ba_ref, nw_ref, o_ref, st_ref, *, tr):
    L = GLA_CHUNK

    @pl.when(pl.program_id(2) == 0)
    def _():
        st_ref[...] = jnp.zeros_like(st_ref)

    row = lax.broadcasted_iota(jnp.int32, (tr, tr), 0)
    col = lax.broadcasted_iota(jnp.int32, (tr, tr), 1)
    same = lax.shift_right_logical(row, 6) == lax.shift_right_logical(col, 6)
    causal = same & (row >= col)
    tri = jnp.where(causal, 1.0, 0.0).astype(BF16)
    ones = jnp.where(same, 1.0, 0.0).astype(BF16)
    lo = lax.broadcasted_iota(jnp.int32, (1, LANE), 1) < GLA_DK
    tn = (((0,), (0,)), ((), ()))
    alr = alr_ref[0]

    for hh in range(GLA_HEADS_PER_STEP):
        lanes = slice(hh * LANE, (hh + 1) * LANE)
        x = jnp.dot(alr, wa_ref[hh], preferred_element_type=F32) + ba_ref[hh]
        log_a = (jnp.minimum(x, 0.0) - jnp.log1p(jnp.exp(-jnp.abs(x)))) * (1.0 / GLA_TAU)
        parts = _split3(log_a)
        b = functools.reduce(jnp.add, [jnp.dot(tri, a, preferred_element_type=F32) for a in parts])
        b_end = functools.reduce(jnp.add, [jnp.dot(ones, a, preferred_element_type=F32) for a in parts])
        blk = qk_ref[0, :, lanes].astype(F32)
        swp = pltpu.roll(blk, 64, 1)
        q_dec = jnp.where(lo, blk * jnp.exp(b) * (GLA_DK ** -0.5), 0.0).astype(BF16)
        k_inv = jnp.where(lo, swp * jnp.exp(-b), 0.0).astype(BF16)
        k_end = jnp.where(lo, swp * jnp.exp(b_end - b), 0.0).astype(BF16)
        decay = jnp.exp(b_end)
        v = v_ref[0, :, lanes]
        attn = lax.dot_general(q_dec, k_inv, _NT, preferred_element_type=F32)
        attn = jnp.where(causal, attn, 0.0).astype(BF16)
        o_intra = jnp.dot(attn, v, preferred_element_type=F32)

        st = st_ref[hh]
        outs = []
        for c in range(tr // L):
            rows = slice(c * L, (c + 1) * L)
            outs.append(o_intra[rows] + lax.dot_general(q_dec[rows], st.astype(BF16), _NT,
                                                        preferred_element_type=F32))
            st = st * decay[c * L:c * L + 1] + lax.dot_general(v[rows], k_end[rows], tn,
                                                               preferred_element_type=F32)
        st_ref[hh] = st
        o = jnp.concatenate(outs, axis=0)
        g = g_ref[0, :, lanes].astype(F32)
        o_ref[0, :, lanes] = (_rms(o, nw_ref[...]) * (g * jax.nn.sigmoid(g))).astype(o_ref.dtype)


def gla(z3, wa, ba, nw, *, tr=256):
    B, S, _ = z3.shape
    tr = min(tr, S)
    H = GLA_HEADS
    hp = GLA_HEADS_PER_STEP
    wide = hp * LANE
    return pl.pallas_call(
        functools.partial(_gla_kernel, tr=tr),
        out_shape=jax.ShapeDtypeStruct((B, S, H * GLA_DV), BF16),
        grid=(B, H // hp, S // tr),
        in_specs=[pl.BlockSpec((1, tr, wide), lambda b, h, r: (b, r, AB_BLK_QK // hp + h)),
                  pl.BlockSpec((1, tr, wide), lambda b, h, r: (b, r, AB_BLK_V // hp + h)),
                  pl.BlockSpec((1, tr, wide), lambda b, h, r: (b, r, AB_BLK_G // hp + h)),
                  pl.BlockSpec((1, tr, LANE), lambda b, h, r: (b, r, AB_BLK_ALR)),
                  pl.BlockSpec((hp, LANE, LANE), lambda b, h, r: (h, 0, 0)),
                  pl.BlockSpec((hp, 1, LANE), lambda b, h, r: (h, 0, 0)),
                  pl.BlockSpec((1, LANE), lambda b, h, r: (0, 0))],
        out_specs=pl.BlockSpec((1, tr, wide), lambda b, h, r: (b, r, h)),
        scratch_shapes=[pltpu.VMEM((hp, GLA_DV, LANE), F32)],
        compiler_params=_cp(("parallel", "parallel", "arbitrary")),
        name="gla",
    )(z3, z3, z3, z3, wa, ba, nw.reshape(1, GLA_DV))


def _mla_proj_kernel(cq_ref, ckv_ref, kr_ref, qnw_ref, kvnw_ref, wq_ref, wkv_ref, c2_ref, s2_ref,
                     q_ref, k_ref, v_ref, cqn_ref, ckvn_ref, kro_ref):
    @pl.when(pl.program_id(1) == 0)
    def _():
        cqn_ref[...] = _rms(cq_ref[...].astype(F32), qnw_ref[...]).astype(BF16)
        ckvn_ref[...] = _rms(ckv_ref[...].astype(F32), kvnw_ref[...]).astype(BF16)
        kr = kr_ref[...].astype(F32)
        kro_ref[...] = (kr * c2_ref[...] + pltpu.roll(kr, 64, 1) * s2_ref[...]).astype(BF16)

    scale = MLA_QK ** -0.5 * LOG2E
    yq = jnp.dot(cqn_ref[...], wq_ref[0], preferred_element_type=F32)
    y2 = yq[:, LANE:]
    qr = y2 * c2_ref[...] + pltpu.roll(y2, 64, 1) * s2_ref[...]
    q_ref[0, 0, :, :MLA_NOPE] = (yq[:, :LANE] * scale).astype(BF16)
    q_ref[0, 0, :, MLA_NOPE:] = (qr[:, :MLA_ROPE] * scale).astype(BF16)
    ykv = jnp.dot(ckvn_ref[...], wkv_ref[0], preferred_element_type=F32)
    k_ref[0, 0, :, :MLA_NOPE] = ykv[:, :LANE].astype(BF16)
    k_ref[0, 0, :, MLA_NOPE:] = kro_ref[:, :MLA_ROPE]
    v_ref[0, 0] = ykv[:, LANE:].astype(BF16)


def mla_proj(z, qnw, kvnw, wq, wkv, c2, s2, B, S, *, tm=1024):
    T = z.shape[0]
    tm = min(tm, S)
    H = MLA_HEADS
    nb = S // tm

    def omap(i, h):
        return (i // nb, h, i % nb, 0)

    return pl.pallas_call(
        _mla_proj_kernel,
        out_shape=[jax.ShapeDtypeStruct((B, H, S, MLA_QK), BF16),
                   jax.ShapeDtypeStruct((B, H, S, MLA_QK), BF16),
                   jax.ShapeDtypeStruct((B, H, S, MLA_V), BF16)],
        grid=(T // tm, H),
        in_specs=[pl.BlockSpec((tm, MLA_Q_RANK), lambda i, h: (i, AB_BLK_CQ // 4)),
                  pl.BlockSpec((tm, MLA_KV_RANK), lambda i, h: (i, AB_BLK_CKV // 4)),
                  pl.BlockSpec((tm, LANE), lambda i, h: (i, AB_BLK_KR)),
                  pl.BlockSpec((1, MLA_Q_RANK), lambda i, h: (0, 0)),
                  pl.BlockSpec((1, MLA_KV_RANK), lambda i, h: (0, 0)),
                  pl.BlockSpec((1, MLA_Q_RANK, 2 * LANE), lambda i, h: (h, 0, 0)),
                  pl.BlockSpec((1, MLA_KV_RANK, 2 * LANE), lambda i, h: (h, 0, 0)),
                  pl.BlockSpec((tm, LANE), lambda i, h: (i, 0)),
                  pl.BlockSpec((tm, LANE), lambda i, h: (i, 0))],
        out_specs=[pl.BlockSpec((1, 1, tm, MLA_QK), omap),
                   pl.BlockSpec((1, 1, tm, MLA_QK), omap),
                   pl.BlockSpec((1, 1, tm, MLA_V), omap)],
        scratch_shapes=[pltpu.VMEM((tm, MLA_Q_RANK), BF16), pltpu.VMEM((tm, MLA_KV_RANK), BF16),
                        pltpu.VMEM((tm, LANE), BF16)],
        compiler_params=_cp(("parallel", "arbitrary")),
        name="mla_proj",
    )(z, z, z, qnw.reshape(1, -1), kvnw.reshape(1, -1), wq, wkv, c2, s2)


_NT = (((1,), (1,)), ((), ()))


def _softmax_tile(s, v, m_ref, l_ref, acc_ref, first):
    M, tk = s.shape
    chunks = [s[:, c * LANE:(c + 1) * LANE] for c in range(tk // LANE)]
    mrow = jnp.max(functools.reduce(jnp.maximum, chunks), axis=-1, keepdims=True)
    if first:
        m_new = jnp.broadcast_to(mrow, (M, LANE))
    else:
        m_prev = m_ref[...]
        m_new = jnp.maximum(m_prev, mrow)
    ps = [jnp.exp2(c - m_new) for c in chunks]
    lsum = functools.reduce(jnp.add, ps)
    p = (jnp.concatenate(ps, axis=1) if len(ps) > 1 else ps[0]).astype(BF16)
    pv = jnp.dot(p, v, preferred_element_type=F32)
    if first:
        l_ref[...] = lsum
        acc_ref[...] = pv
    else:
        alpha = jnp.exp2(m_prev - m_new)
        l_ref[...] = alpha * l_ref[...] + lsum
        acc_ref[...] = alpha * acc_ref[...] + pv
    m_ref[...] = m_new


def _softmax_finish(l_ref, acc_ref):
    return acc_ref[...] / jnp.sum(l_ref[...], axis=-1, keepdims=True)


def _mla_attn_kernel(q_ref, k_ref, v_ref, o_ref, m_ref, l_ref, acc_ref, *, tq, tk):
    qi = pl.program_id(2)
    r = tq // tk
    q = q_ref[0, 0]
    row = lax.broadcasted_iota(jnp.int32, (tq, tk), 0)
    col = lax.broadcasted_iota(jnp.int32, (tq, tk), 1)

    def score(j, d):
        off = pl.multiple_of(j * tk, tk)
        s = lax.dot_general(q, k_ref[0, 0, pl.ds(off, tk), :], _NT, preferred_element_type=F32)
        if d is not None:
            s = jnp.where(col + d * tk <= row, s, NEG)
        return s, v_ref[0, 0, pl.ds(off, tk), :]

    def attend(tiles, first):
        for n, (s, v) in enumerate(tiles):
            _softmax_tile(s, v, m_ref, l_ref, acc_ref, first and n == 0)

    attend([score(qi * r + d, d) for d in range(r)], True)

    def body(i, carry):
        attend([score(2 * i, None), score(2 * i + 1, None)], False)
        return carry

    lax.fori_loop(0, qi * (r // 2), body, 0)
    o_ref[0] = _softmax_finish(l_ref, acc_ref).astype(o_ref.dtype)


def mla_attention(q, k, v, *, tq=512, tk=256):
    B, H, S, _ = q.shape
    tq = min(tq, S)
    tk = min(tk, tq // 2)
    assert tq % (2 * tk) == 0
    return pl.pallas_call(
        functools.partial(_mla_attn_kernel, tq=tq, tk=tk),
        out_shape=jax.ShapeDtypeStruct((B, S, H * MLA_V), BF16),
        grid=(B, H, S // tq),
        in_specs=[pl.BlockSpec((1, 1, tq, MLA_QK), lambda b, h, i: (b, h, i, 0)),
                  pl.BlockSpec((1, 1, S, MLA_QK), lambda b, h, i: (b, h, 0, 0)),
                  pl.BlockSpec((1, 1, S, MLA_V), lambda b, h, i: (b, h, 0, 0))],
        out_specs=pl.BlockSpec((1, tq, MLA_V), lambda b, h, i: (b, i, h)),
        scratch_shapes=[pltpu.VMEM((tq, LANE), F32), pltpu.VMEM((tq, LANE), F32),
                        pltpu.VMEM((tq, MLA_V), F32)],
        compiler_params=_cp(("parallel", "parallel", "arbitrary")),
        name="mla_attention",
    )(q, k, v)


def _nsa_compress_kernel(r_ref, pos_ref, w1_ref, w2_ref, o_ref):
    r = r_ref[0, 0, 0]
    half = r.shape[1]
    nr = r.shape[0]
    a = jnp.dot(r, w1_ref[0, :half, :], preferred_element_type=F32)
    b = jnp.dot(r, w1_ref[0, half:, :], preferred_element_type=F32)
    pos = jnp.broadcast_to(pos_ref[0], (8, 2 * half)).astype(BF16)
    c = jnp.dot(pos, w1_ref[0], preferred_element_type=F32)[0:1, :]
    pre = a + pltpu.roll(b, nr - 1, 0) + c
    o_ref[0, 0, 0] = jnp.dot(jax.nn.gelu(pre).astype(BF16), w2_ref[0],
                             preferred_element_type=F32).astype(o_ref.dtype)


def nsa_compress(r, pos, w1, w2):
    _, B, G, NR, W = r.shape
    dh = NSA_HEAD_DIM
    return pl.pallas_call(
        _nsa_compress_kernel,
        out_shape=jax.ShapeDtypeStruct((2, B, G, NR, dh), BF16),
        grid=(2, B, G),
        in_specs=[pl.BlockSpec((1, 1, 1, NR, W), lambda c, b, g: (c, b, g, 0, 0)),
                  pl.BlockSpec((1, 1, 2 * W), lambda c, b, g: (c, 0, 0)),
                  pl.BlockSpec((1, 2 * W, dh), lambda c, b, g: (c, 0, 0)),
                  pl.BlockSpec((1, dh, dh), lambda c, b, g: (c, 0, 0))],
        out_specs=pl.BlockSpec((1, 1, 1, NR, dh), lambda c, b, g: (c, b, g, 0, 0)),
        compiler_params=_cp(("parallel", "parallel", "parallel")),
        name="nsa_compress",
    )(r, pos, w1, w2)


def _nsa_cmp_kernel(q_ref, kc_ref, vc_ref, ovt_ref, o_ref, sel_ref, *, tq, ns, ksel):
    qi = pl.program_id(2)
    nc = kc_ref.shape[3]
    kc = kc_ref[0, 0, 0]
    vc = vc_ref[0, 0, 0]
    t = qi * tq + lax.broadcasted_iota(jnp.int32, (tq, 1), 0)
    n = lax.broadcasted_iota(jnp.int32, (1, nc), 1)
    ok = (n * NSA_CMP_STRIDE + (NSA_CMP_LEN - 1)) <= t
    nt = (((1,), (1,)), ((), ()))
    psum = jnp.zeros((tq, nc), F32)
    for j in range(NSA_HPG):
        q = q_ref[0, :, j * LANE:(j + 1) * LANE]
        s = lax.dot_general(q, kc, nt, preferred_element_type=F32)
        s = jnp.where(ok, s, NEG)
        e = jnp.where(ok, jnp.exp2(s - jnp.max(s, axis=-1, keepdims=True)), 0.0)
        d = jnp.sum(e, axis=-1, keepdims=True)
        p = e * jnp.where(d > 0.0, 1.0 / d, 0.0)
        o_ref[0, :, j * LANE:(j + 1) * LANE] = jnp.dot(p.astype(BF16), vc,
                                                        preferred_element_type=F32).astype(o_ref.dtype)
        psum = psum + p
    ph = psum.astype(BF16)
    plo = (psum - ph.astype(F32)).astype(BF16)
    ovt = ovt_ref[...]
    imp = (lax.dot_general(ovt, ph, nt, preferred_element_type=F32)
           + lax.dot_general(ovt, plo, nt, preferred_element_type=F32))
    m = lax.broadcasted_iota(jnp.int32, (ns, 1), 0)
    tt = qi * tq + lax.broadcasted_iota(jnp.int32, (1, tq), 1)
    causal = m * NSA_SEL_LEN <= tt
    cur = lax.shift_right_logical(tt, 6)
    forced = (m == 0) | (m == cur) | (m == cur - 1)
    score = jnp.where(causal, jnp.where(forced, FORCE, imp), -FORCE)
    cnt = jnp.zeros((ns, tq), F32)
    for m2 in range(ns):
        r = score[m2:m2 + 1, :]
        ahead = (r > score) | ((r == score) & (m2 < m))
        cnt = cnt + jnp.where(ahead, 1.0, 0.0)
    sel = jnp.where((cnt < float(ksel)) & causal, 1.0, 0.0)
    selp = jnp.concatenate([sel, jnp.zeros((LANE - ns, tq), F32)], axis=0)
    sel_ref[0, 0] = selp.T.astype(sel_ref.dtype)


def nsa_cmp_select(z3, kv_cmp, ovt, *, tq=512):
    B, S, _ = z3.shape
    G = NSA_KV_GROUPS
    tq = min(tq, S)
    ns = S // NSA_SEL_LEN
    nc = kv_cmp.shape[3]
    kern = functools.partial(_nsa_cmp_kernel, tq=tq, ns=ns, ksel=min(NSA_SEL_TOPK, ns))
    return pl.pallas_call(
        kern,
        out_shape=[jax.ShapeDtypeStruct((B, S, NSA_HEADS * NSA_HEAD_DIM), BF16),
                   jax.ShapeDtypeStruct((B, G, S, LANE), BF16)],
        grid=(B, G, S // tq),
        in_specs=[pl.BlockSpec((1, tq, 4 * LANE), lambda b, g, i: (b, i, g)),
                  pl.BlockSpec((1, 1, 1, nc, LANE), lambda b, g, i: (0, b, g, 0, 0)),
                  pl.BlockSpec((1, 1, 1, nc, LANE), lambda b, g, i: (1, b, g, 0, 0)),
                  pl.BlockSpec((ns, nc), lambda b, g, i: (0, 0))],
        out_specs=[pl.BlockSpec((1, tq, 4 * LANE), lambda b, g, i: (b, i, g)),
                   pl.BlockSpec((1, 1, tq, LANE), lambda b, g, i: (b, g, i, 0))],
        compiler_params=_cp(("parallel", "parallel", "parallel")),
        name="nsa_cmp_select",
    )(z3, kv_cmp, kv_cmp, ovt)


def _stack_heads(q_ref, q4_ref, t):
    for h in range(NSA_HPG):
        q4_ref[h * t:(h + 1) * t, :] = q_ref[0, :, h * LANE:(h + 1) * LANE]


def _mask_heads(s, mask, t):
    return jnp.concatenate([jnp.where(mask, s[h * t:(h + 1) * t], NEG) for h in range(NSA_HPG)], axis=0)


def _nsa_sel_kernel(q_ref, k_ref, v_ref, sel_ref, e_ref, o_ref, q4_ref, m_ref, l_ref, acc_ref, *, t):
    qi = pl.program_id(2)
    _stack_heads(q_ref, q4_ref, t)
    sel = sel_ref[0, 0]
    row = lax.broadcasted_iota(jnp.int32, (t, t), 0)
    col = lax.broadcasted_iota(jnp.int32, (t, t), 1)

    def score(j, diag):
        off = pl.multiple_of(j * t, t)
        mask = jnp.dot(sel, e_ref[j], preferred_element_type=F32) > 0.5
        if diag:
            mask = mask & (col <= row)
        s = lax.dot_general(q4_ref[...], k_ref[0, pl.ds(off, t), :], _NT, preferred_element_type=F32)
        return _mask_heads(s, mask, t), v_ref[0, pl.ds(off, t), :]

    def attend(tiles, first):
        for n, (s, v) in enumerate(tiles):
            _softmax_tile(s, v, m_ref, l_ref, acc_ref, first and n == 0)

    attend([score(qi, True)], True)
    odd = qi & 1

    @pl.when(odd == 1)
    def _():
        attend([score(0, False)], False)

    def body(i, carry):
        j = odd + 2 * i
        attend([score(j, False), score(j + 1, False)], False)
        return carry

    lax.fori_loop(0, lax.shift_right_logical(qi, 1), body, 0)
    o = _softmax_finish(l_ref, acc_ref)
    for h in range(NSA_HPG):
        o_ref[0, :, h * LANE:(h + 1) * LANE] = o[h * t:(h + 1) * t].astype(o_ref.dtype)


def _expand_blocks(S, t):
    key = np.arange(S).reshape(S // t, 1, t)
    blk = np.arange(LANE).reshape(1, LANE, 1)
    return jnp.asarray((key // NSA_SEL_LEN == blk).astype(np.float32), dtype=BF16)


def nsa_selected(z3, sel, *, t=256):
    B, S, _ = z3.shape
    G = NSA_KV_GROUPS
    t = min(t, S)
    n = S // t
    big = pl.BlockSpec((1, t, 4 * LANE), lambda b, g, i: (b, i, g))
    return pl.pallas_call(
        functools.partial(_nsa_sel_kernel, t=t),
        out_shape=jax.ShapeDtypeStruct((B, S, NSA_HEADS * NSA_HEAD_DIM), BF16),
        grid=(B, G, n),
        in_specs=[big,
                  pl.BlockSpec((1, S, LANE), lambda b, g, i: (b, 0, NSA_BLK_KS + g)),
                  pl.BlockSpec((1, S, LANE), lambda b, g, i: (b, 0, NSA_BLK_VS + g)),
                  pl.BlockSpec((1, 1, t, LANE), lambda b, g, i: (b, g, i, 0)),
                  pl.BlockSpec((n, LANE, t), lambda b, g, i: (0, 0, 0))],
        out_specs=big,
        scratch_shapes=[pltpu.VMEM((NSA_HPG * t, LANE), BF16), pltpu.VMEM((NSA_HPG * t, LANE), F32),
                        pltpu.VMEM((NSA_HPG * t, LANE), F32), pltpu.VMEM((NSA_HPG * t, LANE), F32)],
        compiler_params=_cp(("parallel", "parallel", "arbitrary")),
        name="nsa_selected",
    )(z3, z3, z3, sel, _expand_blocks(S, t))


def _nsa_win_kernel(q_ref, k_ref, v_ref, oc_ref, os_ref, gl_ref, bg_ref, o_ref, q4_ref, m_ref, l_ref, acc_ref,
                    *, t, nw):
    qi = pl.program_id(2)
    _stack_heads(q_ref, q4_ref, t)
    row = lax.broadcasted_iota(jnp.int32, (t, t), 0)
    col = lax.broadcasted_iota(jnp.int32, (t, t), 1)

    def score(w, mask):
        off = pl.multiple_of((qi - w) * t, t)
        s = lax.dot_general(q4_ref[...], k_ref[0, pl.ds(off, t), :], _NT, preferred_element_type=F32)
        if mask is not None:
            s = _mask_heads(s, mask, t)
        return s, v_ref[0, pl.ds(off, t), :]

    def attend(tiles, first):
        for n, (s, v) in enumerate(tiles):
            _softmax_tile(s, v, m_ref, l_ref, acc_ref, first and n == 0)

    def older(w):
        return score(w, col > row + (w * t - NSA_WINDOW) if (w + 1) * t > NSA_WINDOW else None)

    attend([score(0, col <= row)], True)
    for w in range(1, nw):
        @pl.when(jnp.minimum(qi, nw - 1) == w)
        def _(w=w):
            attend([older(u) for u in range(1, w + 1)], False)

    o_win = _softmax_finish(l_ref, acc_ref)
    gates = jax.nn.sigmoid(gl_ref[0].astype(F32) + bg_ref[0])
    for h in range(NSA_HPG):
        cols = slice(h * LANE, (h + 1) * LANE)
        o = (gates[:, h:h + 1] * oc_ref[0, :, cols].astype(F32)
             + gates[:, 4 + h:5 + h] * os_ref[0, :, cols].astype(F32)
             + gates[:, 8 + h:9 + h] * o_win[h * t:(h + 1) * t])
        o_ref[0, :, cols] = o.astype(o_ref.dtype)


def nsa_window_merge(z3, o_cmp, o_sel, bg, *, t=256):
    B, S, _ = z3.shape
    G = NSA_KV_GROUPS
    t = min(t, S)
    n = S // t
    assert NSA_WINDOW % t == 0
    nw = min(NSA_WINDOW // t + 1, n)
    big = pl.BlockSpec((1, t, 4 * LANE), lambda b, g, i: (b, i, g))
    return pl.pallas_call(
        functools.partial(_nsa_win_kernel, t=t, nw=nw),
        out_shape=jax.ShapeDtypeStruct((B, S, NSA_HEADS * NSA_HEAD_DIM), BF16),
        grid=(B, G, n),
        in_specs=[big,
                  pl.BlockSpec((1, S, LANE), lambda b, g, i: (b, 0, NSA_BLK_KW + g)),
                  pl.BlockSpec((1, S, LANE), lambda b, g, i: (b, 0, NSA_BLK_VW + g)),
                  big, big,
                  pl.BlockSpec((1, t, LANE), lambda b, g, i: (b, i, NSA_BLK_GATE + g)),
                  pl.BlockSpec((1, 1, LANE), lambda b, g, i: (g, 0, 0))],
        out_specs=big,
        scratch_shapes=[pltpu.VMEM((NSA_HPG * t, LANE), BF16), pltpu.VMEM((NSA_HPG * t, LANE), F32),
                        pltpu.VMEM((NSA_HPG * t, LANE), F32), pltpu.VMEM((NSA_HPG * t, LANE), F32)],
        compiler_params=_cp(("parallel", "parallel", "arbitrary")),
        name="nsa_window_merge",
    )(z3, z3, z3, o_cmp, o_sel, z3, bg)


def _rot_half_cols(w):
    half = w.shape[-1] // 2
    return jnp.concatenate([-w[..., half:], w[..., :half]], axis=-1)


def _prep_ab(w_in, w_alpha_up, b_alpha, w_uq, w_ukv):
    D = w_in.shape[0]
    q_g, k_g, v_g, g_g, a_lr, c_q, c_kv, k_r = _split_cols(w_in, AB_SPLITS)
    qk = jnp.concatenate([q_g.reshape(D, GLA_HEADS, GLA_DK), k_g.reshape(D, GLA_HEADS, GLA_DK)],
                         axis=-1).reshape(D, GLA_HEADS * LANE)
    tail = jnp.concatenate([k_r, _rot_half_cols(k_r), a_lr,
                            jnp.zeros((D, 512 - 2 * MLA_ROPE - GLA_GATE_RANK), w_in.dtype)], axis=-1)
    w = jnp.concatenate([qk, v_g, g_g, c_q, c_kv, tail], axis=-1).astype(BF16)
    wa = w_alpha_up.reshape(GLA_GATE_RANK, GLA_HEADS, GLA_DK).transpose(1, 0, 2)
    wa = jnp.concatenate([wa, wa], axis=-1)
    wa = jnp.pad(wa, ((0, 0), (0, LANE - GLA_GATE_RANK), (0, 0))).astype(BF16)
    ba = b_alpha.reshape(GLA_HEADS, 1, GLA_DK)
    ba = jnp.concatenate([ba, ba], axis=-1)
    wq = w_uq.reshape(MLA_Q_RANK, MLA_HEADS, MLA_QK)
    rope = wq[..., MLA_NOPE:]
    wq = jnp.concatenate([wq, _rot_half_cols(rope)], axis=-1).transpose(1, 0, 2).astype(BF16)
    wkv = w_ukv.reshape(MLA_KV_RANK, MLA_HEADS, MLA_NOPE + MLA_V).transpose(1, 0, 2).astype(BF16)
    return w, wa, ba, wq, wkv


def _prep_nsa(w_in, b_gate):
    D = w_in.shape[0]
    q, kc, vc, ks, vs, kw, vw, gl = _split_cols(w_in, NSA_SPLITS)
    G, HG = NSA_KV_GROUPS, NSA_HPG
    glp = gl.reshape(D, G, HG, 3).transpose(0, 1, 3, 2).reshape(D, G, 3 * HG)
    glp = jnp.pad(glp, ((0, 0), (0, 0), (0, LANE - 3 * HG))).reshape(D, G * LANE)
    w = jnp.concatenate([q, kc, ks, kw, vc, vs, vw, glp], axis=-1).astype(BF16)
    bg = b_gate.reshape(G, HG, 3).transpose(0, 2, 1).reshape(G, 1, 3 * HG)
    bg = jnp.pad(bg, ((0, 0), (0, 0), (0, LANE - 3 * HG)))
    return w, bg


def _overlap_t(S):
    nr = S // NSA_CMP_STRIDE
    ns = S // NSA_SEL_LEN
    c_start = np.arange(nr) * NSA_CMP_STRIDE
    c_end = c_start + NSA_CMP_LEN
    s_start = np.arange(ns) * NSA_SEL_LEN
    s_end = s_start + NSA_SEL_LEN
    ov = (c_start[None, :] < s_end[:, None]) & (c_end[None, :] > s_start[:, None])
    return jnp.asarray(ov.astype(np.float32), dtype=BF16)


def gla_mla_mixer(h, pre_w, B, S, c2, s2, w_in, w_alpha_up, b_alpha, gla_norm_w,
                  q_norm_w, w_uq, kv_norm_w, w_ukv):
    w, wa, ba, wq, wkv = _prep_ab(w_in, w_alpha_up, b_alpha, w_uq, w_ukv)
    z = norm_matmul(h, pre_w, w)
    z3 = z.reshape(B, S, AB_Z)
    o_gla = gla(z3, wa, ba, gla_norm_w)
    qm, km, vm = mla_proj(z, q_norm_w, kv_norm_w, wq, wkv, c2, s2, B, S)
    o_mla = mla_attention(qm, km, vm)
    return [o_gla.reshape(B * S, -1), o_mla.reshape(B * S, -1)]


def nsa_mixer(h, pre_w, B, S, c128, s128, w_in, b_gate, cmp_pos, cmp_w1, cmp_w2):
    w, bg = _prep_nsa(w_in, b_gate)
    z = norm_matmul_rope(h, pre_w, w, c128, s128)
    z3 = z.reshape(B, S, NSA_Z)
    G, dh = NSA_KV_GROUPS, NSA_HEAD_DIM

    def blocks(blk):
        t = z3[:, :, blk * LANE:(blk + G) * LANE].reshape(B, S, G, dh).transpose(0, 2, 1, 3)
        return t.reshape(B, G, S // NSA_CMP_STRIDE, NSA_CMP_STRIDE * dh)

    r = jnp.stack([blocks(NSA_BLK_KC), blocks(NSA_BLK_VC)])
    kv_cmp = nsa_compress(r, cmp_pos.reshape(2, 1, NSA_CMP_LEN * dh),
                          cmp_w1.astype(BF16), cmp_w2.astype(BF16))
    o_cmp, sel = nsa_cmp_select(z3, kv_cmp, _overlap_t(S))
    o_sel = nsa_selected(z3, sel)
    o = nsa_window_merge(z3, o_cmp, o_sel, bg)
    return [o.reshape(B * S, -1)]


def kernel(x, p, positions, ln_mix_pre, ln_mix_post, ln_ffn_pre, ln_ffn_post, ab_w_in, gla_w_alpha_up, gla_b_alpha, gla_norm_w, mla_q_norm_w, mla_w_uq, mla_kv_norm_w, mla_w_ukv, ab_w_out, nsa_w_in, nsa_b_gate, nsa_cmp_pos, nsa_cmp_w1, nsa_cmp_w2, nsa_w_out, ffn_w_gate, ffn_w_up, ffn_w_down, ple_w_gate, ple_b_gate, ple_w_proj):
    B, S, D = x.shape
    T = B * S
    depth = p.shape[0]
    c2, s2, c128, s128 = rope_tables(positions)
    h = x.reshape(T, D)
    ab_out, nsa_out = ab_w_out.astype(BF16), nsa_w_out.astype(BF16)
    w_gate, w_up, w_down = ffn_w_gate.astype(BF16), ffn_w_up.astype(BF16), ffn_w_down.astype(BF16)
    ple_gate, ple_proj = ple_w_gate.astype(BF16), ple_w_proj.astype(BF16)
    p3 = p.reshape(depth, T, -1)
    for i in range(depth):
        j = i // 2
        if i % 2 == 0:
            mix = gla_mla_mixer(h, ln_mix_pre[i], B, S, c2, s2, ab_w_in[j], gla_w_alpha_up[j],
                                gla_b_alpha[j], gla_norm_w[j], mla_q_norm_w[j], mla_w_uq[j],
                                mla_kv_norm_w[j], mla_w_ukv[j])
            w_out = ab_out
        else:
            mix = nsa_mixer(h, ln_mix_pre[i], B, S, c128, s128, nsa_w_in[j], nsa_b_gate[j],
                            nsa_cmp_pos[j], nsa_cmp_w1[j], nsa_cmp_w2[j])
            w_out = nsa_out
        h = matmul_parts_norm_residual(mix, w_out, j, h, ln_mix_post[i])
        act = norm_swiglu(h, ln_ffn_pre[i], w_gate, w_up, i)
        h = matmul_norm_residual(act, w_down, i, h, ln_ffn_post[i], tm=1024, tk=512)
        h = ple(h, ple_gate, ple_b_gate[i], p3, ple_proj, i)
    return h.reshape(B, S, D)
```
